```python
import jax, jax.numpy as jnp
from jax import lax
import numpy as np

D_MODEL = 2048
BATCH = 4
SEQ = 4096
DEPTH = 1

CHUNK = 64
Q_BLOCK = 128
ROPE_THETA = 500000.0
EPS = 1e-5

A_HEADS = 8
A_HEAD_DIM = 128
A_ROT_DIM = A_HEAD_DIM // 4
IDX_HEADS = 16
IDX_DIM = 64
IDX_ROT_DIM = IDX_DIM // 4
TOPK_MAX = 256
INDEX_SCALE = (IDX_DIM ** -0.5) * (IDX_HEADS ** -0.5)

B_HEADS = 8
Q_LORA = 512
KV_LORA = 256
QK_NOPE = 128
QK_ROPE = 64
V_HEAD = 128

A_WIDTH = A_HEADS * A_HEAD_DIM
B_WIDTH = B_HEADS * V_HEAD
MIX_WIDTH = A_WIDTH + B_WIDTH

IN_SPLITS = (A_WIDTH, A_WIDTH, A_WIDTH,
             IDX_HEADS * IDX_DIM, IDX_DIM, IDX_HEADS,
             Q_LORA, KV_LORA, QK_ROPE)
IN_WIDTH = sum(IN_SPLITS)

N_EXPERTS = 32
TOP_K_EXPERTS = 4
D_FF = D_MODEL
SWIGLU_LIMIT = 7.0
SWIGLU_ALPHA = 1.702
MOE_BLOCK = 128

ALPHA = (2 * DEPTH) ** 0.25
BETA = (8 * DEPTH) ** -0.25

kernel_name = "hybrid_dsa_mla_moe_deepnorm_adaln"


def layer_norm_plain(x):
    xf = x.astype(jnp.float32)
    mu = jnp.mean(xf, -1, keepdims=True)
    var = jnp.mean(jnp.square(xf - mu), -1, keepdims=True)
    return ((xf - mu) * lax.rsqrt(var + EPS)).astype(x.dtype)


def layer_norm(x, g, b):
    return (layer_norm_plain(x) * g + b).astype(x.dtype)


def rms_norm(x, g):
    xf = x.astype(jnp.float32)
    y = xf * lax.rsqrt(jnp.mean(jnp.square(xf), -1, keepdims=True) + EPS)
    return (y * g).astype(x.dtype)


def apply_rope(x, positions, rot_dim):
    half = rot_dim // 2
    inv_freq = ROPE_THETA ** (-jnp.arange(half, dtype=jnp.float32) / half)
    ang = positions.astype(jnp.float32)[..., None] * inv_freq
    cos = jnp.cos(ang)[:, :, None, :]
    sin = jnp.sin(ang)[:, :, None, :]
    xr = x[..., :rot_dim].astype(jnp.float32)
    x1, x2 = xr[..., :half], xr[..., half:]
    rot = jnp.concatenate([x1 * cos - x2 * sin, x2 * cos + x1 * sin], -1).astype(x.dtype)
    return jnp.concatenate([rot, x[..., rot_dim:]], -1)


def dsa_attention(q, k, v, iq, ik, iw):
    B, S, H, dh = q.shape
    n_chunks = S // CHUNK
    topk = min(TOPK_MAX, S // 4)
    key_chunk = jnp.arange(S) // CHUNK
    gather = jax.vmap(lambda t, i: t[i])

    def to_chunks(t):
        return jnp.moveaxis(t.reshape((B, n_chunks, CHUNK) + t.shape[2:]), 1, 0)

    def one_chunk(args):
        qc, iqc, iwc, ci = args
        rel = jax.nn.relu(jnp.einsum("bqhd,bsd->bqhs", iqc, ik,
                                     preferred_element_type=jnp.float32))
        index_score = jnp.einsum("bqhs,bqh->bqs", rel, iwc.astype(jnp.float32)) * INDEX_SCALE
        visible = key_chunk <= ci
        index_score = jnp.where(visible[None, None, :], index_score, -jnp.inf)
        _, sel = lax.top_k(index_score, topk)
        k_sel = gather(k, sel)
        v_sel = gather(v, sel)
        s = jnp.einsum("bqhd,bqkhd->bhqk", qc, k_sel,
                       preferred_element_type=jnp.float32) * (dh ** -0.5)
        ok = (sel // CHUNK) <= ci
        s = jnp.where(ok[:, None], s, -jnp.inf)
        p = jax.nn.softmax(s, axis=-1).astype(v.dtype)
        return jnp.einsum("bhqk,bqkhd->bqhd", p, v_sel)

    out = lax.map(one_chunk, (to_chunks(q), to_chunks(iq), to_chunks(iw),
                              jnp.arange(n_chunks)))
    return jnp.moveaxis(out, 0, 1).reshape(B, S, H, dh)


def block_causal_attention(q, k, v):
    B, S, H, dq = q.shape
    n_blocks = S // Q_BLOCK
    key_chunk = jnp.arange(S) // CHUNK
    qs = jnp.moveaxis(q.reshape(B, n_blocks, Q_BLOCK, H, dq), 1, 0)

    def one_block(args):
        qb, bi = args
        q_chunk = (bi * Q_BLOCK + jnp.arange(Q_BLOCK)) // CHUNK
        s = jnp.einsum("bqhd,bshd->bhqs", qb, k,
                       preferred_element_type=jnp.float32) * (dq ** -0.5)
        mask = key_chunk[None, :] <= q_chunk[:, None]
        s = jnp.where(mask, s, -jnp.inf)
        p = jax.nn.softmax(s, axis=-1).astype(v.dtype)
        return jnp.einsum("bhqs,bshd->bqhd", p, v)

    out = lax.map(one_block, (qs, jnp.arange(n_blocks)))
    return jnp.moveaxis(out, 0, 1).reshape(B, S, H, v.shape[-1])


def token_mixers(h, positions, w_in, idx_k_norm_g, idx_k_norm_b, q_norm_g, w_q_up,
                 kv_norm_g, w_kv_up, out_norm_a_g, out_norm_b_g, w_out):
    B, S, _ = h.shape
    proj = h @ w_in
    offsets = np.cumsum(IN_SPLITS)[:-1].tolist()
    aq, ak, av, iq, ik, iw, qd, kvd, kr = jnp.split(proj, offsets, axis=-1)

    aq = apply_rope(aq.reshape(B, S, A_HEADS, A_HEAD_DIM), positions, A_ROT_DIM)
    ak = apply_rope(ak.reshape(B, S, A_HEADS, A_HEAD_DIM), positions, A_ROT_DIM)
    av = av.reshape(B, S, A_HEADS, A_HEAD_DIM)
    iq = apply_rope(iq.reshape(B, S, IDX_HEADS, IDX_DIM), positions, IDX_ROT_DIM)
    ik = layer_norm(ik, idx_k_norm_g, idx_k_norm_b)
    ik = apply_rope(ik[:, :, None, :], positions, IDX_ROT_DIM)[:, :, 0, :]
    out_a = dsa_attention(aq, ak, av, iq, ik, iw).reshape(B, S, A_WIDTH)

    cq = rms_norm(qd, q_norm_g)
    qb = (cq @ w_q_up).reshape(B, S, B_HEADS, QK_NOPE + QK_ROPE)
    q_rope = apply_rope(qb[..., QK_NOPE:], positions, QK_ROPE)
    qb = jnp.concatenate([qb[..., :QK_NOPE], q_rope], -1)
    ckv = rms_norm(kvd, kv_norm_g)
    kv = (ckv @ w_kv_up).reshape(B, S, B_HEADS, QK_NOPE + V_HEAD)
    k_rope = apply_rope(kr[:, :, None, :], positions, QK_ROPE)
    kb = jnp.concatenate([kv[..., :QK_NOPE],
                          jnp.broadcast_to(k_rope, (B, S, B_HEADS, QK_ROPE))], -1)
    vb = kv[..., QK_NOPE:]
    out_b = block_causal_attention(qb, kb, vb).reshape(B, S, B_WIDTH)

    merged = jnp.concatenate([rms_norm(out_a, out_norm_a_g),
                              rms_norm(out_b, out_norm_b_g)], -1)
    return merged @ w_out


def moe_ffn(h, w_router, b_router, w_gate_up, b_gate_up, w_down, b_down):
    B, S, D = h.shape
    N = B * S
    xt = h.reshape(N, D)
    logits = (xt @ w_router + b_router).astype(jnp.float32)
    top_logit, top_e = lax.top_k(logits, TOP_K_EXPERTS)
    gates = jax.nn.softmax(top_logit, axis=-1)

    A = N * TOP_K_EXPERTS
    e_flat = top_e.reshape(A)
    tok_flat = jnp.repeat(jnp.arange(N, dtype=jnp.int32), TOP_K_EXPERTS)
    g_flat = gates.reshape(A)
    order = jnp.argsort(e_flat)
    e_sorted, tok_sorted, g_sorted = e_flat[order], tok_flat[order], g_flat[order]

    counts = jnp.zeros((N_EXPERTS,), jnp.int32).at[e_flat].add(1)
    starts = jnp.cumsum(counts) - counts
    padded = (counts + MOE_BLOCK - 1) // MOE_BLOCK * MOE_BLOCK
    pad_ends = jnp.cumsum(padded)
    pad_starts = pad_ends - padded
    dest = pad_starts[e_sorted] + (jnp.arange(A, dtype=jnp.int32) - starts[e_sorted])

    n_blocks = (A + MOE_BLOCK - 1) // MOE_BLOCK + N_EXPERTS
    R = n_blocks * MOE_BLOCK
    row_tok = jnp.full((R,), N, jnp.int32).at[dest].set(tok_sorted)
    row_gate = jnp.zeros((R,), jnp.float32).at[dest].set(g_sorted)
    block_e = jnp.minimum(
        jnp.searchsorted(pad_ends, jnp.arange(n_blocks, dtype=jnp.int32) * MOE_BLOCK,
                         side="right"), N_EXPERTS - 1)
    x_pad = jnp.concatenate([xt, jnp.zeros((1, D), xt.dtype)], 0)

    def one_block(args):
        toks, gts, e = args
        xb = x_pad[toks]
        gu = xb @ w_gate_up[e] + b_gate_up[e]
        g, u = gu[:, ::2], gu[:, 1::2]
        g = jnp.minimum(g, SWIGLU_LIMIT)
        u = jnp.clip(u, -SWIGLU_LIMIT, SWIGLU_LIMIT)
        act = (u + 1.0) * (g * jax.nn.sigmoid(SWIGLU_ALPHA * g))
        y = act @ w_down[e] + b_down[e]
        return y * gts[:, None].astype(y.dtype)

    ys = lax.map(one_block, (row_tok.reshape(n_blocks, MOE_BLOCK),
                             row_gate.reshape(n_blocks, MOE_BLOCK), block_e))
    out = jnp.zeros((N + 1, D), h.dtype).at[row_tok].add(ys.reshape(R, D))
    return out[:N].reshape(B, S, D)


def setup_inputs(seed: int = 0) -> dict:
    key = jax.random.key(seed)
    ks = jax.random.split(key, 25)
    f32 = jnp.float32

    def nrm(k, shape, fan_in, scale=1.0):
        return jax.random.normal(k, shape, f32) * (scale * fan_in ** -0.5)

    def gain(k, shape):
        return 1.0 + 0.01 * jax.random.normal(k, shape, f32)

    def bias(k, shape):
        return 0.01 * jax.random.normal(k, shape, f32)

    L = DEPTH
    return {
        "x": jax.random.normal(ks[0], (BATCH, SEQ, D_MODEL), f32),
        "c": jax.random.normal(ks[1], (BATCH, D_MODEL), f32),
        "positions": jnp.arange(SEQ, dtype=jnp.int32)[None, :]
                     + jax.random.randint(ks[2], (BATCH, 1), 0, 8192, jnp.int32),
        "w_ada": nrm(ks[3], (L, D_MODEL, 6 * D_MODEL), D_MODEL, 0.5),
        "b_ada": bias(ks[4], (L, 6 * D_MODEL)),
        "w_in": nrm(ks[5], (L, D_MODEL, IN_WIDTH), D_MODEL),
        "idx_k_norm_g": gain(ks[6], (L, IDX_DIM)),
        "idx_k_norm_b": bias(ks[7], (L, IDX_DIM)),
        "q_norm_g": gain(ks[8], (L, Q_LORA)),
        "w_q_up": nrm(ks[9], (L, Q_LORA, B_HEADS * (QK_NOPE + QK_ROPE)), Q_LORA),
        "kv_norm_g": gain(ks[10], (L, KV_LORA)),
        "w_kv_up": nrm(ks[11], (L, KV_LORA, B_HEADS * (QK_NOPE + V_HEAD)), KV_LORA),
        "out_norm_a_g": gain(ks[12], (L, A_WIDTH)),
        "out_norm_b_g": gain(ks[13], (L, B_WIDTH)),
        "w_out": nrm(ks[14], (L, MIX_WIDTH, D_MODEL), MIX_WIDTH, BETA),
        "ln_mix_g": gain(ks[15], (L, D_MODEL)),
        "ln_mix_b": bias(ks[16], (L, D_MODEL)),
        "w_router": nrm(ks[17], (L, D_MODEL, N_EXPERTS), D_MODEL),
        "b_router": bias(ks[18], (L, N_EXPERTS)),
        "w_gate_up": nrm(ks[19], (L, N_EXPERTS, D_MODEL, 2 * D_FF), D_MODEL),
        "b_gate_up": bias(ks[20], (L, N_EXPERTS, 2 * D_FF)),
        "w_down": nrm(ks[21], (L, N_EXPERTS, D_FF, D_MODEL), D_FF, BETA),
        "b_down": bias(ks[22], (L, N_EXPERTS, D_MODEL)),
        "ln_ffn_g": gain(ks[23], (L, D_MODEL)),
        "ln_ffn_b": bias(ks[24], (L, D_MODEL)),
    }


def reference(x, c, positions, w_ada, b_ada, w_in, idx_k_norm_g, idx_k_norm_b,
              q_norm_g, w_q_up, kv_norm_g, w_kv_up, out_norm_a_g, out_norm_b_g, w_out,
              ln_mix_g, ln_mix_b, w_router, b_router, w_gate_up, b_gate_up, w_down,
              b_down, ln_ffn_g, ln_ffn_b):
    for l in range(DEPTH):
        ada = jax.nn.silu(c) @ w_ada[l] + b_ada[l]
        sh1, sc1, g1, sh2, sc2, g2 = jnp.split(ada[:, None, :], 6, axis=-1)

        h = layer_norm_plain(x) * (1.0 + sc1) + sh1
        mix = token_mixers(h, positions, w_in[l], idx_k_norm_g[l], idx_k_norm_b[l],
                           q_norm_g[l], w_q_up[l], kv_norm_g[l], w_kv_up[l],
                           out_norm_a_g[l], out_norm_b_g[l], w_out[l])
        x = layer_norm(ALPHA * x + g1 * mix, ln_mix_g[l], ln_mix_b[l])

        h = layer_norm_plain(x) * (1.0 + sc2) + sh2
        ffn = moe_ffn(h, w_router[l], b_router[l], w_gate_up[l], b_gate_up[l],
                      w_down[l], b_down[l])
        x = layer_norm(ALPHA * x + g2 * ffn, ln_ffn_g[l], ln_ffn_b[l])
    return x
```

```python
import functools
import math

import jax
import jax.numpy as jnp
import numpy as np
from jax import lax
from jax.experimental import pallas as pl
from jax.experimental.pallas import tpu as pltpu

CHUNK = 64
ROPE_THETA = 500000.0
EPS = 1e-5
A_HEADS = 8
A_HEAD_DIM = 128
A_ROT_DIM = A_HEAD_DIM // 4
IDX_HEADS = 16
IDX_DIM = 64
IDX_ROT_DIM = IDX_DIM // 4
TOPK_MAX = 256
INDEX_SCALE = (IDX_DIM ** -0.5) * (IDX_HEADS ** -0.5)
B_HEADS = 8
Q_LORA = 512
KV_LORA = 256
QK_NOPE = 128
QK_ROPE = 64
V_HEAD = 128
A_WIDTH = A_HEADS * A_HEAD_DIM
B_WIDTH = B_HEADS * V_HEAD
TOP_K_EXPERTS = 4
SWIGLU_LIMIT = 7.0
SWIGLU_ALPHA = 1.702

LANES = 128
QK_PAD = 256
GROUP_W = 1024
N_GROUPS = 5
VMEM_LIMIT = 56 * 1024 * 1024
NEG_BIG = -1e30
INT_MIN = -2 ** 31

_MXU_DTYPE = jnp.bfloat16


def _cparams(sem, vmem=VMEM_LIMIT):
    return pltpu.CompilerParams(dimension_semantics=sem, vmem_limit_bytes=vmem)


def _tile(n, t):
    t = min(n, t)
    assert n % t == 0, (n, t)
    return t


def _dot(a, b):
    return jnp.dot(a.astype(_MXU_DTYPE), b.astype(_MXU_DTYPE), preferred_element_type=jnp.float32)


def _dot_nt(a, b):
    return lax.dot_general(a.astype(_MXU_DTYPE), b.astype(_MXU_DTYPE), (((1,), (1,)), ((), ())),
                           preferred_element_type=jnp.float32)


def _ln_plain(x):
    mu = jnp.mean(x, axis=-1, keepdims=True)
    d = x - mu
    var = jnp.mean(d * d, axis=-1, keepdims=True)
    return d * lax.rsqrt(var + EPS)


def _rms(x, g):
    return x * lax.rsqrt(jnp.mean(x * x, axis=-1, keepdims=True) + EPS) * g


def _rope128(x, cos, sin, half, group):
    lane = lax.broadcasted_iota(jnp.int32, x.shape, 1) % group
    partner = jnp.where(lane < half, pltpu.roll(x, LANES - half, 1), pltpu.roll(x, half, 1))
    return x * cos + partner * sin


def _ada_kernel(c_ref, w_ref, b_ref, o_ref):
    c = c_ref[...]
    o_ref[...] = _dot(c * jax.nn.sigmoid(c), w_ref[...]) + b_ref[...]


def _ada(c8, w, b):
    d, n = w.shape
    tn = math.gcd(n, 1024)
    return pl.pallas_call(
        _ada_kernel,
        out_shape=jax.ShapeDtypeStruct((8, n), jnp.float32),
        grid=(n // tn,),
        in_specs=[pl.BlockSpec((8, d), lambda j: (0, 0)),
                  pl.BlockSpec((d, tn), lambda j: (0, j)),
                  pl.BlockSpec((1, tn), lambda j: (0, j))],
        out_specs=pl.BlockSpec((8, tn), lambda j: (0, j)),
        compiler_params=_cparams(("arbitrary",)),
        name="ada",
    )(c8, w, b)


def _rope_rows(rot, group):
    half = rot // 2
    inv_freq = ROPE_THETA ** (-jnp.arange(half, dtype=jnp.float32) / half)
    lane = np.arange(LANES) % group
    freq = jnp.where(lane < rot, inv_freq[lane % half], 0.0)
    sign = np.where(lane < half, -1.0, np.where(lane < rot, 1.0, 0.0)).astype(np.float32)
    return freq.reshape(1, LANES).astype(jnp.float32), jnp.asarray(sign).reshape(1, LANES)


def _rope_tables_kernel(pos_ref, fa, sa, fi, si, fm, sm, ca_o, sa_o, ci_o, si_o, cm_o, sm_o):
    pos = pos_ref[...].astype(jnp.float32)
    for f, s, c_o, s_o in ((fa, sa, ca_o, sa_o), (fi, si, ci_o, si_o), (fm, sm, cm_o, sm_o)):
        ang = pos * f[...]
        c_o[...] = jnp.cos(ang)
        s_o[...] = jnp.sin(ang) * s[...]


def _rope_tables(pos_col):
    n = pos_col.shape[0]
    t = _tile(n, 1024)
    rows = (*_rope_rows(A_ROT_DIM, A_HEAD_DIM), *_rope_rows(IDX_ROT_DIM, IDX_DIM), *_rope_rows(QK_ROPE, LANES))
    row_spec = pl.BlockSpec((1, LANES), lambda i: (0, 0))
    tab_spec = pl.BlockSpec((t, LANES), lambda i: (i, 0))
    return pl.pallas_call(
        _rope_tables_kernel,
        out_shape=[jax.ShapeDtypeStruct((n, LANES), jnp.float32)] * 6,
        grid=(n // t,),
        in_specs=[pl.BlockSpec((t, 1), lambda i: (i, 0))] + [row_spec] * 6,
        out_specs=[tab_spec] * 6,
        compiler_params=_cparams(("arbitrary",)),
        name="rope_tables",
    )(pos_col, *rows)


def _proj_kernel(x_ref, sc_ref, sh_ref, w_ref, ca, sa, ci, si, aq_o, ak_o, av_o, iq_o, sm_o, h_scr):
    j = pl.program_id(1)

    @pl.when(j == 0)
    def _():
        h = _ln_plain(x_ref[...]) * (1.0 + sc_ref[0]) + sh_ref[0]
        h_scr[...] = h.astype(h_scr.dtype)

    acc = jnp.dot(h_scr[...], w_ref[...], preferred_element_type=jnp.float32)

    def rope_heads(o_ref, cos_ref, sin_ref, half, group, scale):
        cos, sin = cos_ref[...], sin_ref[...]
        for t in range(GROUP_W // LANES):
            sl = slice(t * LANES, (t + 1) * LANES)
            o_ref[:, sl] = (_rope128(acc[:, sl], cos, sin, half, group) * scale).astype(o_ref.dtype)

    @pl.when(j == 0)
    def _():
        rope_heads(aq_o, ca, sa, A_ROT_DIM // 2, A_HEAD_DIM, A_HEAD_DIM ** -0.5)

    @pl.when(j == 1)
    def _():
        rope_heads(ak_o, ca, sa, A_ROT_DIM // 2, A_HEAD_DIM, 1.0)

    @pl.when(j == 2)
    def _():
        av_o[...] = acc.astype(av_o.dtype)

    @pl.when(j == 3)
    def _():
        rope_heads(iq_o, ci, si, IDX_ROT_DIM // 2, IDX_DIM, 1.0)

    @pl.when(j == 4)
    def _():
        sm_o[...] = acc


def _proj(x2, ada3, w_perm, ca, sa, ci, si, seq):
    n, d = x2.shape
    tm = _tile(seq, 512)
    row = lambda i, j: (i, 0)
    ada_spec = lambda col: pl.BlockSpec((1, 1, d), lambda i, j, col=col: (i * tm // seq, 0, col))
    big = lambda dt: jax.ShapeDtypeStruct((n, GROUP_W), dt)
    return pl.pallas_call(
        _proj_kernel,
        out_shape=[big(_MXU_DTYPE)] * 4 + [big(jnp.float32)],
        grid=(n // tm, N_GROUPS),
        in_specs=[pl.BlockSpec((tm, d), row), ada_spec(1), ada_spec(0),
                  pl.BlockSpec((d, GROUP_W), lambda i, j: (0, j))] + [pl.BlockSpec((tm, LANES), row)] * 4,
        out_specs=[pl.BlockSpec((tm, GROUP_W), row)] * 5,
        scratch_shapes=[pltpu.VMEM((tm, d), _MXU_DTYPE)],
        compiler_params=_cparams(("arbitrary", "arbitrary")),
        name="proj",
    )(x2, ada3, ada3, w_perm, ca, sa, ci, si)


def _mla_prep_kernel(sm_ref, ci, si, cm, sm, qg, kvg, ikg, ikb, wq_ref, wkv_ref,
                     qb_o, kb_o, vb_o, iklo_o, ikhi_o, iw_o):
    small = sm_ref[...]
    qd = small[:, :Q_LORA]
    kvd = small[:, Q_LORA:Q_LORA + KV_LORA]
    ikw = small[:, Q_LORA + KV_LORA:Q_LORA + KV_LORA + LANES]
    kr = small[:, Q_LORA + KV_LORA + LANES:]

    lane = lax.broadcasted_iota(jnp.int32, ikw.shape, 1)
    is_k = lane < IDX_DIM
    ik = jnp.where(is_k, ikw, 0.0)
    mu = jnp.sum(ik, axis=-1, keepdims=True) * (1.0 / IDX_DIM)
    dk = jnp.where(is_k, ikw - mu, 0.0)
    var = jnp.sum(dk * dk, axis=-1, keepdims=True) * (1.0 / IDX_DIM)
    ik = dk * lax.rsqrt(var + EPS) * ikg[...] + ikb[...]
    ik = _rope128(ik, ci[...], si[...], IDX_ROT_DIM // 2, IDX_DIM)
    ik = jnp.where(is_k, ik, 0.0)
    iklo_o[...] = ik.astype(iklo_o.dtype)
    ikhi_o[...] = pltpu.roll(ik, IDX_DIM, 1).astype(ikhi_o.dtype)
    iw_o[...] = pltpu.roll(ikw, LANES - IDX_DIM, 1) * INDEX_SCALE

    cos_m, sin_m = cm[...], sm[...]
    q = _dot(_rms(qd, qg[...]), wq_ref[...])
    scale = (QK_NOPE + QK_ROPE) ** -0.5
    for h in range(B_HEADS):
        lo = slice(h * QK_PAD, h * QK_PAD + LANES)
        hi = slice(h * QK_PAD + LANES, (h + 1) * QK_PAD)
        qb_o[:, lo] = (q[:, lo] * scale).astype(qb_o.dtype)
        qb_o[:, hi] = (_rope128(q[:, hi], cos_m, sin_m, QK_ROPE // 2, LANES) * scale).astype(qb_o.dtype)

    kv = _dot(_rms(kvd, kvg[...]), wkv_ref[...])
    krr = _rope128(kr, cos_m, sin_m, QK_ROPE // 2, LANES).astype(kb_o.dtype)
    for h in range(B_HEADS):
        kb_o[:, h * QK_PAD:h * QK_PAD + LANES] = kv[:, h * QK_NOPE:(h + 1) * QK_NOPE].astype(kb_o.dtype)
        kb_o[:, h * QK_PAD + LANES:(h + 1) * QK_PAD] = krr
    vb_o[...] = kv[:, B_HEADS * QK_NOPE:].astype(vb_o.dtype)


def _mla_prep(small, ci, si, cm, sm, qg, kvg, ikg, ikb, wq, wkv):
    n = small.shape[0]
    tm = _tile(n, 512)
    row = lambda i: (i, 0)
    const = lambda a: pl.BlockSpec(a.shape, lambda i: (0, 0))
    tab = pl.BlockSpec((tm, LANES), row)
    wide = B_HEADS * QK_PAD
    return pl.pallas_call(
        _mla_prep_kernel,
        out_shape=[jax.ShapeDtypeStruct((n, wide), _MXU_DTYPE), jax.ShapeDtypeStruct((n, wide), _MXU_DTYPE),
                   jax.ShapeDtypeStruct((n, B_WIDTH), _MXU_DTYPE),
                   jax.ShapeDtypeStruct((n, LANES), _MXU_DTYPE), jax.ShapeDtypeStruct((n, LANES), _MXU_DTYPE),
                   jax.ShapeDtypeStruct((n, LANES), jnp.float32)],
        grid=(n // tm,),
        in_specs=[pl.BlockSpec((tm, GROUP_W), row), tab, tab, tab, tab,
                  const(qg), const(kvg), const(ikg), const(ikb), const(wq), const(wkv)],
        out_specs=[pl.BlockSpec((tm, wide), row), pl.BlockSpec((tm, wide), row), pl.BlockSpec((tm, B_WIDTH), row),
                   tab, tab, tab],
        compiler_params=_cparams(("arbitrary",)),
        name="mla_prep",
    )(small, ci, si, cm, sm, qg, kvg, ikg, ikb, wq, wkv)


def _indexer_kernel(iq_ref, iklo_ref, ikhi_ref, iw_ref, mask_o, key_scr, *, tq, tk, topk):
    i = pl.program_id(1)
    n_kb_all = mask_o.shape[1]
    n_kb = ((i + 1) * tq + tk - 1) // tk
    q_chunk = (i * tq + lax.broadcasted_iota(jnp.int32, (tq, tk), 0)) // CHUNK
    col = lax.broadcasted_iota(jnp.int32, (tq, tk), 1)
    iw = iw_ref[0]

    def score_block(kb, carry):
        klo = iklo_ref[0, pl.ds(kb * tk, tk), :]
        khi = ikhi_ref[0, pl.ds(kb * tk, tk), :]
        sc = jnp.zeros((tq, tk), jnp.float32)
        for p in range(IDX_HEADS // 2):
            qp = iq_ref[0, :, p * LANES:(p + 1) * LANES]
            for half, kk in ((0, klo), (1, khi)):
                hd = 2 * p + half
                sc = sc + jnp.maximum(_dot_nt(qp, kk), 0.0) * iw[:, hd:hd + 1]
        bits = pltpu.bitcast(sc, jnp.int32)
        key = jnp.where(bits < 0, bits ^ 0x7FFFFFFF, bits)
        visible = (kb * tk + col) // CHUNK <= q_chunk
        key_scr[kb] = jnp.where(visible, key, INT_MIN)
        return carry

    lax.fori_loop(0, n_kb, score_block, 0)

    def bit_pass(it, t_u):
        cand_u = t_u | lax.shift_left(jnp.int32(1), jnp.asarray(31 - it, jnp.int32))
        cand_s = cand_u ^ INT_MIN

        def count_block(kb, acc):
            blk = key_scr[kb]
            for c in range(tk // LANES):
                acc = acc + jnp.where(blk[:, c * LANES:(c + 1) * LANES] >= cand_s, 1.0, 0.0)
            return acc

        acc = lax.fori_loop(0, n_kb, count_block, jnp.zeros((tq, LANES), jnp.float32))
        cnt = jnp.sum(acc, axis=1, keepdims=True)
        return jnp.where(cnt >= topk, cand_u, t_u)

    t_u = lax.fori_loop(0, 32, bit_pass, jnp.zeros((tq, 1), jnp.int32))
    t_s = jnp.maximum(t_u ^ INT_MIN, INT_MIN + 1)

    for kb in range(n_kb_all):
        @pl.when(kb < n_kb)
        def _():
            mask_o[0, kb] = jnp.where(key_scr[kb] >= t_s, 1.0, 0.0).astype(mask_o.dtype)

        @pl.when(kb >= n_kb)
        def _():
            mask_o[0, kb] = jnp.zeros((tq, tk), mask_o.dtype)


def _indexer(iq3, iklo3, ikhi3, iw3, topk):
    b, s, _ = iq3.shape
    tq = _tile(s, 256)
    tk = _tile(s, 512)
    return pl.pallas_call(
        functools.partial(_indexer_kernel, tq=tq, tk=tk, topk=topk),
        out_shape=jax.ShapeDtypeStruct((b, s // tk, s, tk), jnp.int8),
        grid=(b, s // tq),
        in_specs=[pl.BlockSpec((1, tq, GROUP_W), lambda bi, i: (bi, i, 0)),
                  pl.BlockSpec((1, s, LANES), lambda bi, i: (bi, 0, 0)),
                  pl.BlockSpec((1, s, LANES), lambda bi, i: (bi, 0, 0)),
                  pl.BlockSpec((1, tq, LANES), lambda bi, i: (bi, i, 0))],
        out_specs=pl.BlockSpec((1, s // tk, tq, tk), lambda bi, i: (bi, 0, i, 0)),
        scratch_shapes=[pltpu.VMEM((s // tk, tq, tk), jnp.int32)],
        compiler_params=_cparams(("arbitrary", "arbitrary")),
        name="indexer",
    )(iq3, iklo3, ikhi3, iw3)


def _flash_head(q, k_ref, v_ref, h, dqk, dv, n_kb, tk, mask_fn):
    tq = q.shape[0]

    def step(kb, carry):
        m, l, acc = carry
        k = k_ref[0, pl.ds(kb * tk, tk), h * dqk:(h + 1) * dqk]
        v = v_ref[0, pl.ds(kb * tk, tk), h * dv:(h + 1) * dv]
        s = mask_fn(kb, _dot_nt(q, k))
        m_new = jnp.maximum(m, jnp.max(s, axis=1, keepdims=True))
        alpha = jnp.exp(m - m_new)
        p = jnp.exp(s - m_new)
        l = alpha * l + jnp.sum(p, axis=1, keepdims=True)
        acc = alpha * acc + _dot(p, v)
        return m_new, l, acc

    init = (jnp.full((tq, 1), NEG_BIG, jnp.float32), jnp.zeros((tq, 1), jnp.float32),
            jnp.zeros((tq, dv), jnp.float32))
    _, l, acc = lax.fori_loop(0, n_kb, step, init)
    return acc / l


def _attn_a_kernel(q_ref, k_ref, v_ref, mask_ref, g_ref, o_ref, o_scr, *, tq, tk):
    i = pl.program_id(1)
    n_kb = ((i + 1) * tq + tk - 1) // tk

    def mask_fn(kb, s):
        return jnp.where(mask_ref[0, kb].astype(jnp.float32) > 0.0, s, NEG_BIG)

    for h in range(A_HEADS):
        q = q_ref[0, :, h * A_HEAD_DIM:(h + 1) * A_HEAD_DIM]
        o_scr[:, h * A_HEAD_DIM:(h + 1) * A_HEAD_DIM] = _flash_head(
            q, k_ref, v_ref, h, A_HEAD_DIM, A_HEAD_DIM, n_kb, tk, mask_fn)
    o_ref[0] = _rms(o_scr[...], g_ref[...]).astype(o_ref.dtype)


def _attn_a(q3, k3, v3, mask4, g):
    b, s, w = q3.shape
    tk = mask4.shape[3]
    tq = _tile(s, 256)
    full = lambda bi, i: (bi, 0, 0)
    return pl.pallas_call(
        functools.partial(_attn_a_kernel, tq=tq, tk=tk),
        out_shape=jax.ShapeDtypeStruct((b, s, w), _MXU_DTYPE),
        grid=(b, s // tq),
        in_specs=[pl.BlockSpec((1, tq, w), lambda bi, i: (bi, i, 0)),
                  pl.BlockSpec((1, s, w), full, pipeline_mode=pl.Buffered(1)),
                  pl.BlockSpec((1, s, w), full, pipeline_mode=pl.Buffered(1)),
                  pl.BlockSpec((1, s // tk, tq, tk), lambda bi, i: (bi, 0, i, 0)),
                  pl.BlockSpec((1, w), lambda bi, i: (0, 0))],
        out_specs=pl.BlockSpec((1, tq, w), lambda bi, i: (bi, i, 0)),
        scratch_shapes=[pltpu.VMEM((tq, w), jnp.float32)],
        compiler_params=_cparams(("arbitrary", "arbitrary")),
        name="attn_a",
    )(q3, k3, v3, mask4, g)


def _attn_b_kernel(q_ref, k_ref, v_ref, g_ref, o_ref, o_scr, *, tq, tk):
    i = pl.program_id(1)
    n_kb = ((i + 1) * tq + tk - 1) // tk
    q_chunk = (i * tq + lax.broadcasted_iota(jnp.int32, (tq, tk), 0)) // CHUNK
    col = lax.broadcasted_iota(jnp.int32, (tq, tk), 1)

    def mask_fn(kb, s):
        return jnp.where((kb * tk + col) // CHUNK <= q_chunk, s, NEG_BIG)

    for h in range(B_HEADS):
        q = q_ref[0, :, h * QK_PAD:(h + 1) * QK_PAD]
        o_scr[:, h * V_HEAD:(h + 1) * V_HEAD] = _flash_head(q, k_ref, v_ref, h, QK_PAD, V_HEAD, n_kb, tk, mask_fn)
    o_ref[0] = _rms(o_scr[...], g_ref[...]).astype(o_ref.dtype)


def _attn_b(q3, k3, v3, g):
    b, s, wq = q3.shape
    wv = v3.shape[2]
    tq = _tile(s, 256)
    tk = _tile(s, 512)
    full = lambda bi, i: (bi, 0, 0)
    return pl.pallas_call(
        functools.partial(_attn_b_kernel, tq=tq, tk=tk),
        out_shape=jax.ShapeDtypeStruct((b, s, wv), _MXU_DTYPE),
        grid=(b, s // tq),
        in_specs=[pl.BlockSpec((1, tq, wq), lambda bi, i: (bi, i, 0)),
                  pl.BlockSpec((1, s, wq), full, pipeline_mode=pl.Buffered(1)),
                  pl.BlockSpec((1, s, wv), full, pipeline_mode=pl.Buffered(1)),
                  pl.BlockSpec((1, wv), lambda bi, i: (0, 0))],
        out_specs=pl.BlockSpec((1, tq, wv), lambda bi, i: (bi, i, 0)),
        scratch_shapes=[pltpu.VMEM((tq, wv), jnp.float32)],
        compiler_params=_cparams(("arbitrary", "arbitrary")),
        name="attn_b",
    )(q3, k3, v3, g)


def _outproj_kernel(ma_ref, mb_ref, w_ref, x_ref, g1_ref, sc2_ref, sh2_ref, lg_ref, lb_ref, wr_ref, br_ref,
                    x1_o, h2_o, lgt_o, *, alpha, n_experts):
    mix = (jnp.dot(ma_ref[...], w_ref[:A_WIDTH, :], preferred_element_type=jnp.float32)
           + jnp.dot(mb_ref[...], w_ref[A_WIDTH:, :], preferred_element_type=jnp.float32))
    x1 = _ln_plain(alpha * x_ref[...] + g1_ref[0] * mix) * lg_ref[...] + lb_ref[...]
    x1_o[...] = x1
    h2 = _ln_plain(x1) * (1.0 + sc2_ref[0]) + sh2_ref[0]
    for s in range(h2_o.shape[1]):
        h2_o[:, s, :] = h2[:, s * LANES:(s + 1) * LANES]
    logits = jnp.dot(h2, wr_ref[...], preferred_element_type=jnp.float32,
                     precision=lax.Precision.HIGHEST) + br_ref[...]
    lane = lax.broadcasted_iota(jnp.int32, logits.shape, 1)
    lgt_o[...] = jnp.where(lane < n_experts, logits, -jnp.inf)


def _outproj(ma, mb, w_out, x2, ada3, lg, lb, wr, br, seq, alpha, n_experts):
    n, d = x2.shape
    tm = _tile(seq, 256)
    row = lambda i: (i, 0)
    const = lambda a: pl.BlockSpec(a.shape, lambda i: (0,) * a.ndim)
    ada_spec = lambda col: pl.BlockSpec((1, 1, d), lambda i, col=col: (i * tm // seq, 0, col))
    return pl.pallas_call(
        functools.partial(_outproj_kernel, alpha=alpha, n_experts=n_experts),
        out_shape=[jax.ShapeDtypeStruct((n, d), jnp.float32),
                   jax.ShapeDtypeStruct((n, d // LANES, LANES), jnp.float32),
                   jax.ShapeDtypeStruct((n, LANES), jnp.float32)],
        grid=(n // tm,),
        in_specs=[pl.BlockSpec((tm, A_WIDTH), row), pl.BlockSpec((tm, B_WIDTH), row),
                  pl.BlockSpec(w_out.shape, lambda i: (0, 0), pipeline_mode=pl.Buffered(1)),
                  pl.BlockSpec((tm, d), row), ada_spec(2), ada_spec(4), ada_spec(3),
                  const(lg), const(lb), const(wr), const(br)],
        out_specs=[pl.BlockSpec((tm, d), row), pl.BlockSpec((tm, d // LANES, LANES), lambda i: (i, 0, 0)),
                   pl.BlockSpec((tm, LANES), row)],
        compiler_params=_cparams(("arbitrary",)),
        name="outproj",
    )(ma, mb, w_out, x2, ada3, ada3, ada3, lg, lb, wr, br)


def _route_kernel(lgt_ref, top_o, gate_o, rank_o, cnt_o, carry):
    tb = lgt_ref.shape[0]

    @pl.when(pl.program_id(0) == 0)
    def _():
        carry[...] = jnp.zeros_like(carry)

    lane = lax.broadcasted_iota(jnp.int32, (tb, LANES), 1)
    lane_f = lane.astype(jnp.float32)
    work = lgt_ref[...]
    vals, idxs, hots = [], [], []
    for _ in range(TOP_K_EXPERTS):
        m = jnp.max(work, axis=1, keepdims=True)
        idx = jnp.min(jnp.where(work == m, lane_f, float(LANES)), axis=1, keepdims=True)
        hot = lane_f == idx
        vals.append(m)
        idxs.append(idx)
        hots.append(hot)
        work = jnp.where(hot, -jnp.inf, work)

    exps = [jnp.exp(v - vals[0]) for v in vals]
    denom = exps[0] + exps[1] + exps[2] + exps[3]

    member = jnp.zeros((tb, LANES), jnp.float32)
    for hot in hots:
        member = member + jnp.where(hot, 1.0, 0.0)
    r = lax.broadcasted_iota(jnp.int32, (tb, tb), 0)
    c = lax.broadcasted_iota(jnp.int32, (tb, tb), 1)
    before = jnp.where(c < r, 1.0, 0.0).astype(jnp.bfloat16)
    prefix = jnp.dot(before, member.astype(jnp.bfloat16), preferred_element_type=jnp.float32) + carry[...]

    top = jnp.zeros((tb, LANES), jnp.int32)
    gate = jnp.zeros((tb, LANES), jnp.float32)
    rank = jnp.zeros((tb, LANES), jnp.int32)
    for k in range(TOP_K_EXPERTS):
        rk = jnp.sum(jnp.where(hots[k], prefix, 0.0), axis=1, keepdims=True)
        top = jnp.where(lane == k, idxs[k].astype(jnp.int32), top)
        gate = jnp.where(lane == k, exps[k] / denom, gate)
        rank = jnp.where(lane == k, rk.astype(jnp.int32), rank)
    top_o[...] = top
    gate_o[...] = gate
    rank_o[...] = rank
    carry[...] = carry[...] + jnp.sum(member, axis=0, keepdims=True)
    cnt_o[...] = jnp.broadcast_to(carry[...], cnt_o.shape).astype(jnp.int32)


def _route(logits):
    n = logits.shape[0]
    tb = _tile(n, 512)
    row = pl.BlockSpec((tb, LANES), lambda i: (i, 0))
    return pl.pallas_call(
        _route_kernel,
        out_shape=[jax.ShapeDtypeStruct((n, LANES), jnp.int32), jax.ShapeDtypeStruct((n, LANES), jnp.float32),
                   jax.ShapeDtypeStruct((n, LANES), jnp.int32), jax.ShapeDtypeStruct((8, LANES), jnp.int32)],
        grid=(n // tb,),
        in_specs=[row],
        out_specs=[row, row, row, pl.BlockSpec((8, LANES), lambda i: (0, 0))],
        scratch_shapes=[pltpu.VMEM((1, LANES), jnp.float32)],
        compiler_params=_cparams(("arbitrary",)),
        name="route",
    )(logits)


def _row_copy(src, s_row, dst, d_row, sem):
    return pltpu.make_async_copy(src.at[s_row], dst.at[d_row], sem)


def _dispatch_kernel(dest_ref, h_hbm, xs_in, xs_hbm, sem, *, td):
    del xs_in
    base = pl.program_id(0) * td

    def issue(t, c):
        for k in range(TOP_K_EXPERTS):
            _row_copy(h_hbm, base + t, xs_hbm, dest_ref[(base + t) * TOP_K_EXPERTS + k], sem).start()
        return c

    lax.fori_loop(0, td, issue, 0)

    def drain(t, c):
        for k in range(TOP_K_EXPERTS):
            _row_copy(h_hbm, 0, xs_hbm, 0, sem).wait()
        return c

    lax.fori_loop(0, td, drain, 0)


def _dispatch(dest_flat, h3, rows):
    n, sub, _ = h3.shape
    td = _tile(n, 256)
    xs0 = jnp.zeros((rows, sub, LANES), h3.dtype)
    return pl.pallas_call(
        functools.partial(_dispatch_kernel, td=td),
        out_shape=jax.ShapeDtypeStruct(xs0.shape, xs0.dtype),
        grid_spec=pltpu.PrefetchScalarGridSpec(
            num_scalar_prefetch=1, grid=(n // td,),
            in_specs=[pl.BlockSpec(memory_space=pl.ANY), pl.BlockSpec(memory_space=pl.ANY)],
            out_specs=pl.BlockSpec(memory_space=pl.ANY),
            scratch_shapes=[pltpu.SemaphoreType.DMA(())]),
        input_output_aliases={2: 0},
        compiler_params=pltpu.CompilerParams(dimension_semantics=("arbitrary",), has_side_effects=True),
        name="dispatch",
    )(dest_flat, h3, xs0)


def _moe_gemm_kernel(be_ref, nu_ref, xs_ref, wgu_ref, bgu_ref, wd_ref, bd_ref, y_o, xb_scr, acc_scr):
    j = pl.program_id(0)
    f = pl.program_id(1)
    sub = xs_ref.shape[1]
    tf2 = wgu_ref.shape[2]

    @pl.when(j < nu_ref[0])
    def _():
        @pl.when(f == 0)
        def _():
            for s in range(sub):
                xb_scr[:, s * LANES:(s + 1) * LANES] = xs_ref[:, s, :].astype(xb_scr.dtype)
            acc_scr[...] = jnp.broadcast_to(bd_ref[0], acc_scr.shape)

        gu = jnp.dot(xb_scr[...], wgu_ref[0].astype(_MXU_DTYPE), preferred_element_type=jnp.float32) + bgu_ref[0]
        nxt = pltpu.roll(gu, tf2 - 1, 1)
        g = jnp.minimum(gu, SWIGLU_LIMIT)
        u = jnp.clip(nxt, -SWIGLU_LIMIT, SWIGLU_LIMIT)
        act = ((u + 1.0) * (g * jax.nn.sigmoid(SWIGLU_ALPHA * g))).astype(_MXU_DTYPE)
        r = lax.broadcasted_iota(jnp.int32, (2 * LANES, LANES), 0)
        c = lax.broadcasted_iota(jnp.int32, (2 * LANES, LANES), 1)
        pick = jnp.where(r == 2 * c, 1.0, 0.0).astype(_MXU_DTYPE)
        parts = [jnp.dot(act[:, t * 2 * LANES:(t + 1) * 2 * LANES], pick, preferred_element_type=jnp.float32)
                 for t in range(tf2 // (2 * LANES))]
        actc = jnp.concatenate(parts, axis=1).astype(_MXU_DTYPE)
        acc_scr[...] += jnp.dot(actc, wd_ref[0].astype(_MXU_DTYPE), preferred_element_type=jnp.float32)

        @pl.when(f == pl.num_programs(1) - 1)
        def _():
            for s in range(sub):
                y_o[:, s, :] = acc_scr[:, s * LANES:(s + 1) * LANES]


def _moe_gemm(block_e, n_used, xs, wgu, bgu3, wd, bd3, bm):
    rows, sub, _ = xs.shape
    e, d, ff2 = wgu.shape
    ff = ff2 // 2
    tf = _tile(ff, 256)
    nblk = rows // bm
    blk = lambda j, f, be, nu: (jnp.minimum(j, nu[0] - 1), 0, 0)
    ex = lambda j, be, nu: be[jnp.minimum(j, nu[0] - 1)]
    return pl.pallas_call(
        _moe_gemm_kernel,
        out_shape=jax.ShapeDtypeStruct(xs.shape, jnp.float32),
        grid_spec=pltpu.PrefetchScalarGridSpec(
            num_scalar_prefetch=2, grid=(nblk, ff // tf),
            in_specs=[pl.BlockSpec((bm, sub, LANES), blk),
                      pl.BlockSpec((1, d, 2 * tf), lambda j, f, be, nu: (ex(j, be, nu), 0, f)),
                      pl.BlockSpec((1, 1, 2 * tf), lambda j, f, be, nu: (ex(j, be, nu), 0, f)),
                      pl.BlockSpec((1, tf, d), lambda j, f, be, nu: (ex(j, be, nu), f, 0)),
                      pl.BlockSpec((1, 1, d), lambda j, f, be, nu: (ex(j, be, nu), 0, 0))],
            out_specs=pl.BlockSpec((bm, sub, LANES), blk),
            scratch_shapes=[pltpu.VMEM((bm, d), _MXU_DTYPE), pltpu.VMEM((bm, d), jnp.float32)]),
        input_output_aliases={2: 0},
        compiler_params=_cparams(("arbitrary", "arbitrary")),
        name="moe_gemm",
    )(block_e, n_used, xs, wgu, bgu3, wd, bd3)


def _combine_kernel(dest_ref, y_hbm, gate_ref, x1_ref, g2_ref, lg_ref, lb_ref, o_ref, buf, sem, *, tc, alpha):
    base = pl.program_id(0) * tc
    sub = buf.shape[1]

    def issue(t, c):
        for k in range(TOP_K_EXPERTS):
            _row_copy(y_hbm, dest_ref[(base + t) * TOP_K_EXPERTS + k], buf, k * tc + t, sem).start()
        return c

    lax.fori_loop(0, tc, issue, 0)

    def drain(t, c):
        for k in range(TOP_K_EXPERTS):
            _row_copy(y_hbm, 0, buf, 0, sem).wait()
        return c

    lax.fori_loop(0, tc, drain, 0)

    gate = gate_ref[...]
    pieces = []
    for s in range(sub):
        acc = jnp.zeros((tc, LANES), jnp.float32)
        for k in range(TOP_K_EXPERTS):
            acc = acc + buf[k * tc:(k + 1) * tc, s, :] * gate[:, k:k + 1]
        pieces.append(acc)
    ffn = jnp.concatenate(pieces, axis=1)
    o_ref[...] = _ln_plain(alpha * x1_ref[...] + g2_ref[0] * ffn) * lg_ref[...] + lb_ref[...]


def _combine(dest_flat, y, gates, x1, ada3, lg, lb, seq, alpha):
    n, d = x1.shape
    sub = d // LANES
    tc = _tile(seq, 128)
    row = lambda i, dr: (i, 0)
    return pl.pallas_call(
        functools.partial(_combine_kernel, tc=tc, alpha=alpha),
        out_shape=jax.ShapeDtypeStruct((n, d), jnp.float32),
        grid_spec=pltpu.PrefetchScalarGridSpec(
            num_scalar_prefetch=1, grid=(n // tc,),
            in_specs=[pl.BlockSpec(memory_space=pl.ANY),
                      pl.BlockSpec((tc, LANES), row), pl.BlockSpec((tc, d), row),
                      pl.BlockSpec((1, 1, d), lambda i, dr: (i * tc // seq, 0, 5)),
                      pl.BlockSpec((1, d), lambda i, dr: (0, 0)), pl.BlockSpec((1, d), lambda i, dr: (0, 0))],
            out_specs=pl.BlockSpec((tc, d), row),
            scratch_shapes=[pltpu.VMEM((TOP_K_EXPERTS * tc, sub, LANES), jnp.float32),
                            pltpu.SemaphoreType.DMA(())]),
        compiler_params=_cparams(("arbitrary",)),
        name="combine",
    )(dest_flat, y, gates, x1, ada3, lg, lb)


def _pad_cols(a, width):
    return jnp.pad(a, ((0, 0), (0, width - a.shape[1])))


def _perm_w_in(w_in):
    o = np.cumsum([0, A_WIDTH, A_WIDTH, A_WIDTH, IDX_HEADS * IDX_DIM, IDX_DIM, IDX_HEADS, Q_LORA, KV_LORA, QK_ROPE])
    seg = lambda i: w_in[:, o[i]:o[i + 1]]
    ikw = _pad_cols(jnp.concatenate([seg(4), seg(5)], axis=1), LANES)
    kr = _pad_cols(seg(8), LANES)
    return jnp.concatenate([seg(0), seg(1), seg(2), seg(3), seg(6), seg(7), ikw, kr], axis=1)


def _perm_w_q_up(w):
    w = w.reshape(Q_LORA, B_HEADS, QK_NOPE + QK_ROPE)
    w = jnp.pad(w, ((0, 0), (0, 0), (0, QK_PAD - QK_NOPE - QK_ROPE)))
    return w.reshape(Q_LORA, B_HEADS * QK_PAD)


def _perm_w_kv_up(w):
    w = w.reshape(KV_LORA, B_HEADS, QK_NOPE + V_HEAD)
    return jnp.concatenate([w[:, :, :QK_NOPE].reshape(KV_LORA, -1), w[:, :, QK_NOPE:].reshape(KV_LORA, -1)], axis=1)


def _moe_block_rows(n_tokens):
    return 512 if n_tokens >= 4096 else 128


def kernel(x, c, positions, w_ada, b_ada, w_in, idx_k_norm_g, idx_k_norm_b, q_norm_g, w_q_up, kv_norm_g, w_kv_up,
           out_norm_a_g, out_norm_b_g, w_out, ln_mix_g, ln_mix_b, w_router, b_router, w_gate_up, b_gate_up,
           w_down, b_down, ln_ffn_g, ln_ffn_b):
    bsz, seq, d = x.shape
    depth = w_ada.shape[0]
    n_experts = w_router.shape[2]
    n = bsz * seq
    alpha = (2 * depth) ** 0.25
    topk = min(TOPK_MAX, seq // 4)
    bm = _moe_block_rows(n)
    n_assign = n * TOP_K_EXPERTS
    nblk = n_assign // bm + n_experts
    rows = nblk * bm

    ca, sa, ci, si, cm, sm = _rope_tables(positions.reshape(n, 1))
    c8 = jnp.pad(c, ((0, 8 - bsz), (0, 0)))
    x2 = x.reshape(n, d)
    row = lambda a: a.reshape(1, -1)

    for l in range(depth):
        ada3 = _ada(c8, w_ada[l], row(b_ada[l]))[:bsz].reshape(bsz, 1, 6 * d)

        aq, ak, av, iq, small = _proj(x2, ada3, _perm_w_in(w_in[l]).astype(_MXU_DTYPE), ca, sa, ci, si, seq)
        qb, kb, vb, iklo, ikhi, iw = _mla_prep(
            small, ci, si, cm, sm, row(q_norm_g[l]), row(kv_norm_g[l]),
            _pad_cols(row(idx_k_norm_g[l]), LANES), _pad_cols(row(idx_k_norm_b[l]), LANES),
            _perm_w_q_up(w_q_up[l]).astype(_MXU_DTYPE), _perm_w_kv_up(w_kv_up[l]).astype(_MXU_DTYPE))
        b3 = lambda a: a.reshape(bsz, seq, a.shape[-1])
        mask = _indexer(b3(iq), b3(iklo), b3(ikhi), b3(iw), topk)
        out_a = _attn_a(b3(aq), b3(ak), b3(av), mask, row(out_norm_a_g[l]))
        out_b = _attn_b(b3(qb), b3(kb), b3(vb), row(out_norm_b_g[l]))

        x1, h3, logits = _outproj(
            out_a.reshape(n, A_WIDTH), out_b.reshape(n, B_WIDTH), w_out[l].astype(_MXU_DTYPE), x2, ada3,
            row(ln_mix_g[l]), row(ln_mix_b[l]), _pad_cols(w_router[l], LANES), _pad_cols(row(b_router[l]), LANES),
            seq, alpha, n_experts)
        top, gates, rank, counts = _route(logits)

        cnt = counts[0, :n_experts]
        padded = (cnt + bm - 1) // bm * bm
        pad_ends = jnp.cumsum(padded)
        pad_starts = pad_ends - padded
        dest = (pad_starts[top[:, :TOP_K_EXPERTS]] + rank[:, :TOP_K_EXPERTS]).reshape(-1).astype(jnp.int32)
        block_e = jnp.minimum(
            jnp.searchsorted(pad_ends, jnp.arange(nblk, dtype=jnp.int32) * bm, side="right"),
            n_experts - 1).astype(jnp.int32)
        n_used = (pad_ends[-1:] // bm).astype(jnp.int32)

        xs = _dispatch(dest, h3, rows)
        y = _moe_gemm(block_e, n_used, xs, w_gate_up[l], b_gate_up[l].reshape(n_experts, 1, -1), w_down[l],
                      b_down[l].reshape(n_experts, 1, -1), bm)
        x2 = _combine(dest, y, gates, x1, ada3, row(ln_ffn_g[l]), row(ln_ffn_b[l]), seq, alpha)

    return x2.reshape(bsz, seq, d)
```

```python
import functools
import math

import jax
import jax.numpy as jnp
import numpy as np
from jax import lax
from jax.experimental import pallas as pl
from jax.experimental.pallas import tpu as pltpu

CHUNK = 64
ROPE_THETA = 500000.0
EPS = 1e-5
A_HEADS = 8
A_HEAD_DIM = 128
A_ROT_DIM = A_HEAD_DIM // 4
IDX_HEADS = 16
IDX_DIM = 64
IDX_ROT_DIM = IDX_DIM // 4
TOPK_MAX = 256
INDEX_SCALE = (IDX_DIM ** -0.5) * (IDX_HEADS ** -0.5)
B_HEADS = 8
Q_LORA = 512
KV_LORA = 256
QK_NOPE = 128
QK_ROPE = 64
V_HEAD = 128
A_WIDTH = A_HEADS * A_HEAD_DIM
B_WIDTH = B_HEADS * V_HEAD
TOP_K_EXPERTS = 4
SWIGLU_LIMIT = 7.0
SWIGLU_ALPHA = 1.702

LANES = 128
QK_PAD = 256
GROUP_W = 1024
N_GROUPS = 5
VMEM_LIMIT = 56 * 1024 * 1024
NEG_BIG = -1e30
ROW_DMA_UNROLL = 8
INT_MIN = -2 ** 31

_MXU_DTYPE = jnp.bfloat16


def _cparams(sem, vmem=VMEM_LIMIT):
    return pltpu.CompilerParams(dimension_semantics=sem, vmem_limit_bytes=vmem)


def _tile(n, t):
    t = min(n, t)
    assert n % t == 0, (n, t)
    return t


def _dot(a, b):
    return jnp.dot(a.astype(_MXU_DTYPE), b.astype(_MXU_DTYPE), preferred_element_type=jnp.float32)


def _dot_nt(a, b):
    return lax.dot_general(a.astype(_MXU_DTYPE), b.astype(_MXU_DTYPE), (((1,), (1,)), ((), ())),
                           preferred_element_type=jnp.float32)


def _ln_plain(x):
    mu = jnp.mean(x, axis=-1, keepdims=True)
    d = x - mu
    var = jnp.mean(d * d, axis=-1, keepdims=True)
    return d * lax.rsqrt(var + EPS)


def _rms(x, g):
    return x * lax.rsqrt(jnp.mean(x * x, axis=-1, keepdims=True) + EPS) * g


def _rope128(x, cos, sin, half, group):
    lane = lax.broadcasted_iota(jnp.int32, x.shape, 1) % group
    partner = jnp.where(lane < half, pltpu.roll(x, LANES - half, 1), pltpu.roll(x, half, 1))
    return x * cos + partner * sin


def _ada_kernel(c_ref, w_ref, b_ref, o_ref):
    c = c_ref[...]
    o_ref[...] = _dot(c * jax.nn.sigmoid(c), w_ref[...]) + b_ref[...]


def _ada(c8, w, b):
    d, n = w.shape
    tn = math.gcd(n, 1024)
    return pl.pallas_call(
        _ada_kernel,
        out_shape=jax.ShapeDtypeStruct((8, n), jnp.float32),
        grid=(n // tn,),
        in_specs=[pl.BlockSpec((8, d), lambda j: (0, 0)),
                  pl.BlockSpec((d, tn), lambda j: (0, j)),
                  pl.BlockSpec((1, tn), lambda j: (0, j))],
        out_specs=pl.BlockSpec((8, tn), lambda j: (0, j)),
        compiler_params=_cparams(("arbitrary",)),
        name="ada",
    )(c8, w, b)


def _rope_rows(rot, group):
    half = rot // 2
    inv_freq = ROPE_THETA ** (-jnp.arange(half, dtype=jnp.float32) / half)
    lane = np.arange(LANES) % group
    freq = jnp.where(lane < rot, inv_freq[lane % half], 0.0)
    sign = np.where(lane < half, -1.0, np.where(lane < rot, 1.0, 0.0)).astype(np.float32)
    return freq.reshape(1, LANES).astype(jnp.float32), jnp.asarray(sign).reshape(1, LANES)


def _rope_tables_kernel(pos_ref, fa, sa, fi, si, fm, sm, ca_o, sa_o, ci_o, si_o, cm_o, sm_o):
    pos = pos_ref[...].astype(jnp.float32)
    for f, s, c_o, s_o in ((fa, sa, ca_o, sa_o), (fi, si, ci_o, si_o), (fm, sm, cm_o, sm_o)):
        ang = pos * f[...]
        c_o[...] = jnp.cos(ang)
        s_o[...] = jnp.sin(ang) * s[...]


def _rope_tables(pos_col):
    n = pos_col.shape[0]
    t = _tile(n, 1024)
    rows = (*_rope_rows(A_ROT_DIM, A_HEAD_DIM), *_rope_rows(IDX_ROT_DIM, IDX_DIM), *_rope_rows(QK_ROPE, LANES))
    row_spec = pl.BlockSpec((1, LANES), lambda i: (0, 0))
    tab_spec = pl.BlockSpec((t, LANES), lambda i: (i, 0))
    return pl.pallas_call(
        _rope_tables_kernel,
        out_shape=[jax.ShapeDtypeStruct((n, LANES), jnp.float32)] * 6,
        grid=(n // t,),
        in_specs=[pl.BlockSpec((t, 1), lambda i: (i, 0))] + [row_spec] * 6,
        out_specs=[tab_spec] * 6,
        compiler_params=_cparams(("arbitrary",)),
        name="rope_tables",
    )(pos_col, *rows)


def _proj_kernel(x_ref, sc_ref, sh_ref, w_ref, ca, sa, ci, si, aq_o, ak_o, av_o, iq_o, sm_o, h_scr):
    j = pl.program_id(1)

    @pl.when(j == 0)
    def _():
        h = _ln_plain(x_ref[...]) * (1.0 + sc_ref[0]) + sh_ref[0]
        h_scr[...] = h.astype(h_scr.dtype)

    acc = jnp.dot(h_scr[...], w_ref[...], preferred_element_type=jnp.float32)

    def rope_heads(o_ref, cos_ref, sin_ref, half, group, scale):
        cos, sin = cos_ref[...], sin_ref[...]
        for t in range(GROUP_W // LANES):
            sl = slice(t * LANES, (t + 1) * LANES)
            o_ref[:, sl] = (_rope128(acc[:, sl], cos, sin, half, group) * scale).astype(o_ref.dtype)

    @pl.when(j == 0)
    def _():
        rope_heads(aq_o, ca, sa, A_ROT_DIM // 2, A_HEAD_DIM, A_HEAD_DIM ** -0.5)

    @pl.when(j == 1)
    def _():
        rope_heads(ak_o, ca, sa, A_ROT_DIM // 2, A_HEAD_DIM, 1.0)

    @pl.when(j == 2)
    def _():
        av_o[...] = acc.astype(av_o.dtype)

    @pl.when(j == 3)
    def _():
        rope_heads(iq_o, ci, si, IDX_ROT_DIM // 2, IDX_DIM, 1.0)

    @pl.when(j == 4)
    def _():
        sm_o[...] = acc


def _proj(x2, ada3, w_perm, ca, sa, ci, si, seq):
    n, d = x2.shape
    tm = _tile(seq, 512)
    row = lambda i, j: (i, 0)
    ada_spec = lambda col: pl.BlockSpec((1, 1, d), lambda i, j, col=col: (i * tm // seq, 0, col))
    big = lambda dt: jax.ShapeDtypeStruct((n, GROUP_W), dt)
    return pl.pallas_call(
        _proj_kernel,
        out_shape=[big(_MXU_DTYPE)] * 4 + [big(jnp.float32)],
        grid=(n // tm, N_GROUPS),
        in_specs=[pl.BlockSpec((tm, d), row), ada_spec(1), ada_spec(0),
                  pl.BlockSpec((d, GROUP_W), lambda i, j: (0, j))] + [pl.BlockSpec((tm, LANES), row)] * 4,
        out_specs=[pl.BlockSpec((tm, GROUP_W), row)] * 5,
        scratch_shapes=[pltpu.VMEM((tm, d), _MXU_DTYPE)],
        compiler_params=_cparams(("arbitrary", "arbitrary")),
        name="proj",
    )(x2, ada3, ada3, w_perm, ca, sa, ci, si)


def _mla_prep_kernel(sm_ref, ci, si, cm, sm, qg, kvg, ikg, ikb, wq_ref, wkv_ref,
                     qb_o, kb_o, vb_o, iklo_o, ikhi_o, iw_o):
    small = sm_ref[...]
    qd = small[:, :Q_LORA]
    kvd = small[:, Q_LORA:Q_LORA + KV_LORA]
    ikw = small[:, Q_LORA + KV_LORA:Q_LORA + KV_LORA + LANES]
    kr = small[:, Q_LORA + KV_LORA + LANES:]

    lane = lax.broadcasted_iota(jnp.int32, ikw.shape, 1)
    is_k = lane < IDX_DIM
    ik = jnp.where(is_k, ikw, 0.0)
    mu = jnp.sum(ik, axis=-1, keepdims=True) * (1.0 / IDX_DIM)
    dk = jnp.where(is_k, ikw - mu, 0.0)
    var = jnp.sum(dk * dk, axis=-1, keepdims=True) * (1.0 / IDX_DIM)
    ik = dk * lax.rsqrt(var + EPS) * ikg[...] + ikb[...]
    ik = _rope128(ik, ci[...], si[...], IDX_ROT_DIM // 2, IDX_DIM)
    ik = jnp.where(is_k, ik, 0.0)
    iklo_o[...] = ik.astype(iklo_o.dtype)
    ikhi_o[...] = pltpu.roll(ik, IDX_DIM, 1).astype(ikhi_o.dtype)
    iw_o[...] = pltpu.roll(ikw, LANES - IDX_DIM, 1) * INDEX_SCALE

    cos_m, sin_m = cm[...], sm[...]
    q = _dot(_rms(qd, qg[...]), wq_ref[...])
    scale = (QK_NOPE + QK_ROPE) ** -0.5
    for h in range(B_HEADS):
        lo = slice(h * QK_PAD, h * QK_PAD + LANES)
        hi = slice(h * QK_PAD + LANES, (h + 1) * QK_PAD)
        qb_o[:, lo] = (q[:, lo] * scale).astype(qb_o.dtype)
        qb_o[:, hi] = (_rope128(q[:, hi], cos_m, sin_m, QK_ROPE // 2, LANES) * scale).astype(qb_o.dtype)

    kv = _dot(_rms(kvd, kvg[...]), wkv_ref[...])
    krr = _rope128(kr, cos_m, sin_m, QK_ROPE // 2, LANES).astype(kb_o.dtype)
    for h in range(B_HEADS):
        kb_o[:, h * QK_PAD:h * QK_PAD + LANES] = kv[:, h * QK_NOPE:(h + 1) * QK_NOPE].astype(kb_o.dtype)
        kb_o[:, h * QK_PAD + LANES:(h + 1) * QK_PAD] = krr
    vb_o[...] = kv[:, B_HEADS * QK_NOPE:].astype(vb_o.dtype)


def _mla_prep(small, ci, si, cm, sm, qg, kvg, ikg, ikb, wq, wkv):
    n = small.shape[0]
    tm = _tile(n, 512)
    row = lambda i: (i, 0)
    const = lambda a: pl.BlockSpec(a.shape, lambda i: (0, 0))
    tab = pl.BlockSpec((tm, LANES), row)
    wide = B_HEADS * QK_PAD
    return pl.pallas_call(
        _mla_prep_kernel,
        out_shape=[jax.ShapeDtypeStruct((n, wide), _MXU_DTYPE), jax.ShapeDtypeStruct((n, wide), _MXU_DTYPE),
                   jax.ShapeDtypeStruct((n, B_WIDTH), _MXU_DTYPE),
                   jax.ShapeDtypeStruct((n, LANES), _MXU_DTYPE), jax.ShapeDtypeStruct((n, LANES), _MXU_DTYPE),
                   jax.ShapeDtypeStruct((n, LANES), jnp.float32)],
        grid=(n // tm,),
        in_specs=[pl.BlockSpec((tm, GROUP_W), row), tab, tab, tab, tab,
                  const(qg), const(kvg), const(ikg), const(ikb), const(wq), const(wkv)],
        out_specs=[pl.BlockSpec((tm, wide), row), pl.BlockSpec((tm, wide), row), pl.BlockSpec((tm, B_WIDTH), row),
                   tab, tab, tab],
        compiler_params=_cparams(("arbitrary",)),
        name="mla_prep",
    )(small, ci, si, cm, sm, qg, kvg, ikg, ikb, wq, wkv)


def _indexer_kernel(iq_ref, iklo_ref, ikhi_ref, iw_ref, mask_o, key_scr, *, tq, tk, topk):
    i = pl.program_id(1)
    n_kb_all = mask_o.shape[1]
    n_kb = ((i + 1) * tq + tk - 1) // tk
    q_chunk = (i * tq + lax.broadcasted_iota(jnp.int32, (tq, tk), 0)) // CHUNK
    col = lax.broadcasted_iota(jnp.int32, (tq, tk), 1)
    iw = iw_ref[0]

    def score_block(kb, carry):
        klo = iklo_ref[0, pl.ds(kb * tk, tk), :]
        khi = ikhi_ref[0, pl.ds(kb * tk, tk), :]
        sc = jnp.zeros((tq, tk), jnp.float32)
        for p in range(IDX_HEADS // 2):
            qp = iq_ref[0, :, p * LANES:(p + 1) * LANES]
            for half, kk in ((0, klo), (1, khi)):
                hd = 2 * p + half
                sc = sc + jnp.maximum(_dot_nt(qp, kk), 0.0) * iw[:, hd:hd + 1]
        bits = pltpu.bitcast(sc, jnp.int32)
        key = jnp.where(bits < 0, bits ^ 0x7FFFFFFF, bits)
        visible = (kb * tk + col) // CHUNK <= q_chunk
        key_scr[kb] = jnp.where(visible, key, INT_MIN)
        return carry

    lax.fori_loop(0, n_kb, score_block, 0)

    def bit_pass(it, t_u):
        cand_u = t_u | lax.shift_left(jnp.int32(1), jnp.asarray(31 - it, jnp.int32))
        cand_s = cand_u ^ INT_MIN

        def count_block(kb, acc):
            blk = key_scr[kb]
            for c in range(tk // LANES):
                acc = acc + jnp.where(blk[:, c * LANES:(c + 1) * LANES] >= cand_s, 1.0, 0.0)
            return acc

        acc = lax.fori_loop(0, n_kb, count_block, jnp.zeros((tq, LANES), jnp.float32))
        cnt = jnp.sum(acc, axis=1, keepdims=True)
        return jnp.where(cnt >= topk, cand_u, t_u)

    t_u = lax.fori_loop(0, 32, bit_pass, jnp.zeros((tq, 1), jnp.int32))
    t_s = jnp.maximum(t_u ^ INT_MIN, INT_MIN + 1)

    for kb in range(n_kb_all):
        @pl.when(kb < n_kb)
        def _():
            mask_o[0, kb] = jnp.where(key_scr[kb] >= t_s, 1.0, 0.0).astype(mask_o.dtype)

        @pl.when(kb >= n_kb)
        def _():
            mask_o[0, kb] = jnp.zeros((tq, tk), mask_o.dtype)


def _indexer(iq3, iklo3, ikhi3, iw3, topk):
    b, s, _ = iq3.shape
    tq = _tile(s, 256)
    tk = _tile(s, 512)
    return pl.pallas_call(
        functools.partial(_indexer_kernel, tq=tq, tk=tk, topk=topk),
        out_shape=jax.ShapeDtypeStruct((b, s // tk, s, tk), jnp.int8),
        grid=(b, s // tq),
        in_specs=[pl.BlockSpec((1, tq, GROUP_W), lambda bi, i: (bi, i, 0)),
                  pl.BlockSpec((1, s, LANES), lambda bi, i: (bi, 0, 0)),
                  pl.BlockSpec((1, s, LANES), lambda bi, i: (bi, 0, 0)),
                  pl.BlockSpec((1, tq, LANES), lambda bi, i: (bi, i, 0))],
        out_specs=pl.BlockSpec((1, s // tk, tq, tk), lambda bi, i: (bi, 0, i, 0)),
        scratch_shapes=[pltpu.VMEM((s // tk, tq, tk), jnp.int32)],
        compiler_params=_cparams(("arbitrary", "arbitrary")),
        name="indexer",
    )(iq3, iklo3, ikhi3, iw3)


def _flash_head(q, k_ref, v_ref, h, dqk, dv, n_kb, tk, mask_fn):
    tq = q.shape[0]

    def step(kb, carry):
        m, l, acc = carry
        k = k_ref[0, pl.ds(kb * tk, tk), h * dqk:(h + 1) * dqk]
        v = v_ref[0, pl.ds(kb * tk, tk), h * dv:(h + 1) * dv]
        s = mask_fn(kb, _dot_nt(q, k))
        m_new = jnp.maximum(m, jnp.max(s, axis=1, keepdims=True))
        alpha = jnp.exp(m - m_new)
        p = jnp.exp(s - m_new)
        l = alpha * l + jnp.sum(p, axis=1, keepdims=True)
        acc = alpha * acc + _dot(p, v)
        return m_new, l, acc

    init = (jnp.full((tq, 1), NEG_BIG, jnp.float32), jnp.zeros((tq, 1), jnp.float32),
            jnp.zeros((tq, dv), jnp.float32))
    _, l, acc = lax.fori_loop(0, n_kb, step, init)
    return acc / l


def _attn_a_kernel(q_ref, k_ref, v_ref, mask_ref, g_ref, o_ref, o_scr, *, tq, tk):
    i = pl.program_id(1)
    n_kb = ((i + 1) * tq + tk - 1) // tk

    def mask_fn(kb, s):
        return jnp.where(mask_ref[0, kb].astype(jnp.float32) > 0.0, s, NEG_BIG)

    for h in range(A_HEADS):
        q = q_ref[0, :, h * A_HEAD_DIM:(h + 1) * A_HEAD_DIM]
        o_scr[:, h * A_HEAD_DIM:(h + 1) * A_HEAD_DIM] = _flash_head(
            q, k_ref, v_ref, h, A_HEAD_DIM, A_HEAD_DIM, n_kb, tk, mask_fn)
    o_ref[0] = _rms(o_scr[...], g_ref[...]).astype(o_ref.dtype)


def _attn_a(q3, k3, v3, mask4, g):
    b, s, w = q3.shape
    tk = mask4.shape[3]
    tq = _tile(s, 256)
    full = lambda bi, i: (bi, 0, 0)
    return pl.pallas_call(
        functools.partial(_attn_a_kernel, tq=tq, tk=tk),
        out_shape=jax.ShapeDtypeStruct((b, s, w), _MXU_DTYPE),
        grid=(b, s // tq),
        in_specs=[pl.BlockSpec((1, tq, w), lambda bi, i: (bi, i, 0)),
                  pl.BlockSpec((1, s, w), full, pipeline_mode=pl.Buffered(1)),
                  pl.BlockSpec((1, s, w), full, pipeline_mode=pl.Buffered(1)),
                  pl.BlockSpec((1, s // tk, tq, tk), lambda bi, i: (bi, 0, i, 0)),
                  pl.BlockSpec((1, w), lambda bi, i: (0, 0))],
        out_specs=pl.BlockSpec((1, tq, w), lambda bi, i: (bi, i, 0)),
        scratch_shapes=[pltpu.VMEM((tq, w), jnp.float32)],
        compiler_params=_cparams(("arbitrary", "arbitrary")),
        name="attn_a",
    )(q3, k3, v3, mask4, g)


def _attn_b_kernel(q_ref, k_ref, v_ref, g_ref, o_ref, o_scr, *, tq, tk):
    i = pl.program_id(1)
    n_kb = ((i + 1) * tq + tk - 1) // tk
    q_chunk = (i * tq + lax.broadcasted_iota(jnp.int32, (tq, tk), 0)) // CHUNK
    col = lax.broadcasted_iota(jnp.int32, (tq, tk), 1)

    def mask_fn(kb, s):
        return jnp.where((kb * tk + col) // CHUNK <= q_chunk, s, NEG_BIG)

    for h in range(B_HEADS):
        q = q_ref[0, :, h * QK_PAD:(h + 1) * QK_PAD]
        o_scr[:, h * V_HEAD:(h + 1) * V_HEAD] = _flash_head(q, k_ref, v_ref, h, QK_PAD, V_HEAD, n_kb, tk, mask_fn)
    o_ref[0] = _rms(o_scr[...], g_ref[...]).astype(o_ref.dtype)


def _attn_b(q3, k3, v3, g):
    b, s, wq = q3.shape
    wv = v3.shape[2]
    tq = _tile(s, 256)
    tk = _tile(s, 512)
    full = lambda bi, i: (bi, 0, 0)
    return pl.pallas_call(
        functools.partial(_attn_b_kernel, tq=tq, tk=tk),
        out_shape=jax.ShapeDtypeStruct((b, s, wv), _MXU_DTYPE),
        grid=(b, s // tq),
        in_specs=[pl.BlockSpec((1, tq, wq), lambda bi, i: (bi, i, 0)),
                  pl.BlockSpec((1, s, wq), full, pipeline_mode=pl.Buffered(1)),
                  pl.BlockSpec((1, s, wv), full, pipeline_mode=pl.Buffered(1)),
                  pl.BlockSpec((1, wv), lambda bi, i: (0, 0))],
        out_specs=pl.BlockSpec((1, tq, wv), lambda bi, i: (bi, i, 0)),
        scratch_shapes=[pltpu.VMEM((tq, wv), jnp.float32)],
        compiler_params=_cparams(("arbitrary", "arbitrary")),
        name="attn_b",
    )(q3, k3, v3, g)


def _outproj_kernel(ma_ref, mb_ref, w_ref, x_ref, g1_ref, sc2_ref, sh2_ref, lg_ref, lb_ref, wr_ref, br_ref,
                    x1_o, h2_o, lgt_o, *, alpha, n_experts):
    mix = (jnp.dot(ma_ref[...], w_ref[:A_WIDTH, :], preferred_element_type=jnp.float32)
           + jnp.dot(mb_ref[...], w_ref[A_WIDTH:, :], preferred_element_type=jnp.float32))
    x1 = _ln_plain(alpha * x_ref[...] + g1_ref[0] * mix) * lg_ref[...] + lb_ref[...]
    x1_o[...] = x1
    h2 = _ln_plain(x1) * (1.0 + sc2_ref[0]) + sh2_ref[0]
    h2_o[...] = h2
    logits = jnp.dot(h2, wr_ref[...], preferred_element_type=jnp.float32,
                     precision=lax.Precision.HIGHEST) + br_ref[...]
    lane = lax.broadcasted_iota(jnp.int32, logits.shape, 1)
    lgt_o[...] = jnp.where(lane < n_experts, logits, -jnp.inf)


def _outproj(ma, mb, w_out, x2, ada3, lg, lb, wr, br, seq, alpha, n_experts):
    n, d = x2.shape
    tm = _tile(seq, 256)
    row = lambda i: (i, 0)
    const = lambda a: pl.BlockSpec(a.shape, lambda i: (0,) * a.ndim)
    ada_spec = lambda col: pl.BlockSpec((1, 1, d), lambda i, col=col: (i * tm // seq, 0, col))
    return pl.pallas_call(
        functools.partial(_outproj_kernel, alpha=alpha, n_experts=n_experts),
        out_shape=[jax.ShapeDtypeStruct((n, d), jnp.float32),
                   jax.ShapeDtypeStruct((n, d), jnp.float32),
                   jax.ShapeDtypeStruct((n, LANES), jnp.float32)],
        grid=(n // tm,),
        in_specs=[pl.BlockSpec((tm, A_WIDTH), row), pl.BlockSpec((tm, B_WIDTH), row),
                  pl.BlockSpec(w_out.shape, lambda i: (0, 0), pipeline_mode=pl.Buffered(1)),
                  pl.BlockSpec((tm, d), row), ada_spec(2), ada_spec(4), ada_spec(3),
                  const(lg), const(lb), const(wr), const(br)],
        out_specs=[pl.BlockSpec((tm, d), row), pl.BlockSpec((tm, d), row), pl.BlockSpec((tm, LANES), row)],
        compiler_params=_cparams(("arbitrary",)),
        name="outproj",
    )(ma, mb, w_out, x2, ada3, ada3, ada3, lg, lb, wr, br)


def _route_kernel(lgt_ref, top_o, gate_o, rank_o, cnt_o, carry):
    tb = lgt_ref.shape[0]

    @pl.when(pl.program_id(0) == 0)
    def _():
        carry[...] = jnp.zeros_like(carry)

    lane = lax.broadcasted_iota(jnp.int32, (tb, LANES), 1)
    lane_f = lane.astype(jnp.float32)
    work = lgt_ref[...]
    vals, idxs, hots = [], [], []
    for _ in range(TOP_K_EXPERTS):
        m = jnp.max(work, axis=1, keepdims=True)
        idx = jnp.min(jnp.where(work == m, lane_f, float(LANES)), axis=1, keepdims=True)
        hot = lane_f == idx
        vals.append(m)
        idxs.append(idx)
        hots.append(hot)
        work = jnp.where(hot, -jnp.inf, work)

    exps = [jnp.exp(v - vals[0]) for v in vals]
    denom = exps[0] + exps[1] + exps[2] + exps[3]

    member = jnp.zeros((tb, LANES), jnp.float32)
    for hot in hots:
        member = member + jnp.where(hot, 1.0, 0.0)
    r = lax.broadcasted_iota(jnp.int32, (tb, tb), 0)
    c = lax.broadcasted_iota(jnp.int32, (tb, tb), 1)
    before = jnp.where(c < r, 1.0, 0.0).astype(jnp.bfloat16)
    prefix = jnp.dot(before, member.astype(jnp.bfloat16), preferred_element_type=jnp.float32) + carry[...]

    top = jnp.zeros((tb, LANES), jnp.int32)
    gate = jnp.zeros((tb, LANES), jnp.float32)
    rank = jnp.zeros((tb, LANES), jnp.int32)
    for k in range(TOP_K_EXPERTS):
        rk = jnp.sum(jnp.where(hots[k], prefix, 0.0), axis=1, keepdims=True)
        top = jnp.where(lane == k, idxs[k].astype(jnp.int32), top)
        gate = jnp.where(lane == k, exps[k] / denom, gate)
        rank = jnp.where(lane == k, rk.astype(jnp.int32), rank)
    top_o[...] = top
    gate_o[...] = gate
    rank_o[...] = rank
    carry[...] = carry[...] + jnp.sum(member, axis=0, keepdims=True)
    cnt_o[...] = jnp.broadcast_to(carry[...], cnt_o.shape).astype(jnp.int32)


def _route(logits):
    n = logits.shape[0]
    tb = _tile(n, 512)
    row = pl.BlockSpec((tb, LANES), lambda i: (i, 0))
    return pl.pallas_call(
        _route_kernel,
        out_shape=[jax.ShapeDtypeStruct((n, LANES), jnp.int32), jax.ShapeDtypeStruct((n, LANES), jnp.float32),
                   jax.ShapeDtypeStruct((n, LANES), jnp.int32), jax.ShapeDtypeStruct((8, LANES), jnp.int32)],
        grid=(n // tb,),
        in_specs=[row],
        out_specs=[row, row, row, pl.BlockSpec((8, LANES), lambda i: (0, 0))],
        scratch_shapes=[pltpu.VMEM((1, LANES), jnp.float32)],
        compiler_params=_cparams(("arbitrary",)),
        name="route",
    )(logits)


def _row_copy(src, s_row, dst, d_row, sem):
    return pltpu.make_async_copy(src.at[pl.ds(s_row, 1), :], dst.at[pl.ds(d_row, 1), :], sem)


def _dispatch_kernel(dest_ref, h_ref, xs_in, xs_hbm, sem, *, td):
    del xs_in
    base = pl.program_id(0) * td

    def issue(t, c):
        for k in range(TOP_K_EXPERTS):
            _row_copy(h_ref, t, xs_hbm, dest_ref[(base + t) * TOP_K_EXPERTS + k], sem).start()
        return c

    lax.fori_loop(0, td, issue, 0, unroll=ROW_DMA_UNROLL)
    landed = xs_hbm.at[pl.ds(0, td * TOP_K_EXPERTS), :]
    pltpu.make_async_copy(landed, landed, sem).wait()


def _dispatch(dest_flat, h2, rows):
    n, d = h2.shape
    td = _tile(n, 256)
    xs0 = jnp.zeros((rows, d), h2.dtype)
    return pl.pallas_call(
        functools.partial(_dispatch_kernel, td=td),
        out_shape=jax.ShapeDtypeStruct(xs0.shape, xs0.dtype),
        grid_spec=pltpu.PrefetchScalarGridSpec(
            num_scalar_prefetch=1, grid=(n // td,),
            in_specs=[pl.BlockSpec((td, d), lambda i, dr: (i, 0)), pl.BlockSpec(memory_space=pl.ANY)],
            out_specs=pl.BlockSpec(memory_space=pl.ANY),
            scratch_shapes=[pltpu.SemaphoreType.DMA(())]),
        input_output_aliases={2: 0},
        compiler_params=pltpu.CompilerParams(dimension_semantics=("arbitrary",), has_side_effects=True,
                                             disable_bounds_checks=True),
        name="dispatch",
    )(dest_flat, h2, xs0)


def _moe_gemm_kernel(be_ref, nu_ref, xs_ref, wgu_ref, bgu_ref, wd_ref, bd_ref, y_o, xb_scr, acc_scr):
    j = pl.program_id(0)
    f = pl.program_id(1)
    tf2 = wgu_ref.shape[2]

    @pl.when(j < nu_ref[0])
    def _():
        @pl.when(f == 0)
        def _():
            xb_scr[...] = xs_ref[...].astype(xb_scr.dtype)
            acc_scr[...] = jnp.broadcast_to(bd_ref[0], acc_scr.shape)

        gu = jnp.dot(xb_scr[...], wgu_ref[0].astype(_MXU_DTYPE), preferred_element_type=jnp.float32) + bgu_ref[0]
        nxt = pltpu.roll(gu, tf2 - 1, 1)
        g = jnp.minimum(gu, SWIGLU_LIMIT)
        u = jnp.clip(nxt, -SWIGLU_LIMIT, SWIGLU_LIMIT)
        act = ((u + 1.0) * (g * jax.nn.sigmoid(SWIGLU_ALPHA * g))).astype(_MXU_DTYPE)
        r = lax.broadcasted_iota(jnp.int32, (2 * LANES, LANES), 0)
        c = lax.broadcasted_iota(jnp.int32, (2 * LANES, LANES), 1)
        pick = jnp.where(r == 2 * c, 1.0, 0.0).astype(_MXU_DTYPE)
        parts = [jnp.dot(act[:, t * 2 * LANES:(t + 1) * 2 * LANES], pick, preferred_element_type=jnp.float32)
                 for t in range(tf2 // (2 * LANES))]
        actc = jnp.concatenate(parts, axis=1).astype(_MXU_DTYPE)
        acc_scr[...] += jnp.dot(actc, wd_ref[0].astype(_MXU_DTYPE), preferred_element_type=jnp.float32)

        @pl.when(f == pl.num_programs(1) - 1)
        def _():
            y_o[...] = acc_scr[...]


def _moe_gemm(block_e, n_used, xs, wgu, bgu3, wd, bd3, bm):
    rows = xs.shape[0]
    e, d, ff2 = wgu.shape
    ff = ff2 // 2
    tf = _tile(ff, 256)
    nblk = rows // bm
    blk = lambda j, f, be, nu: (jnp.minimum(j, nu[0] - 1), 0)
    ex = lambda j, be, nu: be[jnp.minimum(j, nu[0] - 1)]
    return pl.pallas_call(
        _moe_gemm_kernel,
        out_shape=jax.ShapeDtypeStruct(xs.shape, jnp.float32),
        grid_spec=pltpu.PrefetchScalarGridSpec(
            num_scalar_prefetch=2, grid=(nblk, ff // tf),
            in_specs=[pl.BlockSpec((bm, d), blk),
                      pl.BlockSpec((1, d, 2 * tf), lambda j, f, be, nu: (ex(j, be, nu), 0, f)),
                      pl.BlockSpec((1, 1, 2 * tf), lambda j, f, be, nu: (ex(j, be, nu), 0, f)),
                      pl.BlockSpec((1, tf, d), lambda j, f, be, nu: (ex(j, be, nu), f, 0)),
                      pl.BlockSpec((1, 1, d), lambda j, f, be, nu: (ex(j, be, nu), 0, 0))],
            out_specs=pl.BlockSpec((bm, d), blk),
            scratch_shapes=[pltpu.VMEM((bm, d), _MXU_DTYPE), pltpu.VMEM((bm, d), jnp.float32)]),
        input_output_aliases={2: 0},
        compiler_params=_cparams(("arbitrary", "arbitrary")),
        name="moe_gemm",
    )(block_e, n_used, xs, wgu, bgu3, wd, bd3)


def _combine_kernel(dest_ref, y_hbm, gate_ref, x1_ref, g2_ref, lg_ref, lb_ref, o_ref, buf, sem, *, tc, alpha):
    base = pl.program_id(0) * tc

    def issue(t, c):
        for k in range(TOP_K_EXPERTS):
            _row_copy(y_hbm, dest_ref[(base + t) * TOP_K_EXPERTS + k], buf, k * tc + t, sem).start()
        return c

    lax.fori_loop(0, tc, issue, 0, unroll=ROW_DMA_UNROLL)
    pltpu.make_async_copy(y_hbm.at[pl.ds(0, tc * TOP_K_EXPERTS), :], buf, sem).wait()

    gate = gate_ref[...]
    ffn = buf[0:tc, :] * gate[:, 0:1]
    for k in range(1, TOP_K_EXPERTS):
        ffn = ffn + buf[k * tc:(k + 1) * tc, :] * gate[:, k:k + 1]
    o_ref[...] = _ln_plain(alpha * x1_ref[...] + g2_ref[0] * ffn) * lg_ref[...] + lb_ref[...]


def _combine(dest_flat, y, gates, x1, ada3, lg, lb, seq, alpha):
    n, d = x1.shape
    tc = _tile(seq, 256)
    row = lambda i, dr: (i, 0)
    return pl.pallas_call(
        functools.partial(_combine_kernel, tc=tc, alpha=alpha),
        out_shape=jax.ShapeDtypeStruct((n, d), jnp.float32),
        grid_spec=pltpu.PrefetchScalarGridSpec(
            num_scalar_prefetch=1, grid=(n // tc,),
            in_specs=[pl.BlockSpec(memory_space=pl.ANY),
                      pl.BlockSpec((tc, LANES), row), pl.BlockSpec((tc, d), row),
                      pl.BlockSpec((1, 1, d), lambda i, dr: (i * tc // seq, 0, 5)),
                      pl.BlockSpec((1, d), lambda i, dr: (0, 0)), pl.BlockSpec((1, d), lambda i, dr: (0, 0))],
            out_specs=pl.BlockSpec((tc, d), row),
            scratch_shapes=[pltpu.VMEM((TOP_K_EXPERTS * tc, d), jnp.float32),
                            pltpu.SemaphoreType.DMA(())]),
        compiler_params=pltpu.CompilerParams(dimension_semantics=("arbitrary",), vmem_limit_bytes=VMEM_LIMIT,
                                             disable_bounds_checks=True),
        name="combine",
    )(dest_flat, y, gates, x1, ada3, lg, lb)


def _pad_cols(a, width):
    return jnp.pad(a, ((0, 0), (0, width - a.shape[1])))


def _perm_w_in(w_in):
    o = np.cumsum([0, A_WIDTH, A_WIDTH, A_WIDTH, IDX_HEADS * IDX_DIM, IDX_DIM, IDX_HEADS, Q_LORA, KV_LORA, QK_ROPE])
    seg = lambda i: w_in[:, o[i]:o[i + 1]]
    ikw = _pad_cols(jnp.concatenate([seg(4), seg(5)], axis=1), LANES)
    kr = _pad_cols(seg(8), LANES)
    return jnp.concatenate([seg(0), seg(1), seg(2), seg(3), seg(6), seg(7), ikw, kr], axis=1)


def _perm_w_q_up(w):
    w = w.reshape(Q_LORA, B_HEADS, QK_NOPE + QK_ROPE)
    w = jnp.pad(w, ((0, 0), (0, 0), (0, QK_PAD - QK_NOPE - QK_ROPE)))
    return w.reshape(Q_LORA, B_HEADS * QK_PAD)


def _perm_w_kv_up(w):
    w = w.reshape(KV_LORA, B_HEADS, QK_NOPE + V_HEAD)
    return jnp.concatenate([w[:, :, :QK_NOPE].reshape(KV_LORA, -1), w[:, :, QK_NOPE:].reshape(KV_LORA, -1)], axis=1)


def _moe_block_rows(n_tokens):
    return 512 if n_tokens >= 4096 else 128


def kernel(x, c, positions, w_ada, b_ada, w_in, idx_k_norm_g, idx_k_norm_b, q_norm_g, w_q_up, kv_norm_g, w_kv_up,
           out_norm_a_g, out_norm_b_g, w_out, ln_mix_g, ln_mix_b, w_router, b_router, w_gate_up, b_gate_up,
           w_down, b_down, ln_ffn_g, ln_ffn_b):
    bsz, seq, d = x.shape
    depth = w_ada.shape[0]
    n_experts = w_router.shape[2]
    n = bsz * seq
    alpha = (2 * depth) ** 0.25
    topk = min(TOPK_MAX, seq // 4)
    bm = _moe_block_rows(n)
    n_assign = n * TOP_K_EXPERTS
    nblk = n_assign // bm + n_experts
    rows = nblk * bm

    ca, sa, ci, si, cm, sm = _rope_tables(positions.reshape(n, 1))
    c8 = jnp.pad(c, ((0, 8 - bsz), (0, 0)))
    x2 = x.reshape(n, d)
    row = lambda a: a.reshape(1, -1)

    for l in range(depth):
        ada3 = _ada(c8, w_ada[l], row(b_ada[l]))[:bsz].reshape(bsz, 1, 6 * d)

        aq, ak, av, iq, small = _proj(x2, ada3, _perm_w_in(w_in[l]).astype(_MXU_DTYPE), ca, sa, ci, si, seq)
        qb, kb, vb, iklo, ikhi, iw = _mla_prep(
            small, ci, si, cm, sm, row(q_norm_g[l]), row(kv_norm_g[l]),
            _pad_cols(row(idx_k_norm_g[l]), LANES), _pad_cols(row(idx_k_norm_b[l]), LANES),
            _perm_w_q_up(w_q_up[l]).astype(_MXU_DTYPE), _perm_w_kv_up(w_kv_up[l]).astype(_MXU_DTYPE))
        b3 = lambda a: a.reshape(bsz, seq, a.shape[-1])
        mask = _indexer(b3(iq), b3(iklo), b3(ikhi), b3(iw), topk)
        out_a = _attn_a(b3(aq), b3(ak), b3(av), mask, row(out_norm_a_g[l]))
        out_b = _attn_b(b3(qb), b3(kb), b3(vb), row(out_norm_b_g[l]))

        x1, h2, logits = _outproj(
            out_a.reshape(n, A_WIDTH), out_b.reshape(n, B_WIDTH), w_out[l].astype(_MXU_DTYPE), x2, ada3,
            row(ln_mix_g[l]), row(ln_mix_b[l]), _pad_cols(w_router[l], LANES), _pad_cols(row(b_router[l]), LANES),
            seq, alpha, n_experts)
        top, gates, rank, counts = _route(logits)

        cnt = counts[0, :n_experts]
        padded = (cnt + bm - 1) // bm * bm
        pad_ends = jnp.cumsum(padded)
        pad_starts = pad_ends - padded
        dest = (pad_starts[top[:, :TOP_K_EXPERTS]] + rank[:, :TOP_K_EXPERTS]).reshape(-1).astype(jnp.int32)
        blk_start = jnp.arange(nblk, dtype=jnp.int32) * bm
        block_e = jnp.minimum(jnp.sum((pad_ends[None, :] <= blk_start[:, None]).astype(jnp.int32), axis=1),
                              n_experts - 1).astype(jnp.int32)
        n_used = (pad_ends[-1:] // bm).astype(jnp.int32)

        xs = _dispatch(dest, h2, rows)
        y = _moe_gemm(block_e, n_used, xs, w_gate_up[l], b_gate_up[l].reshape(n_experts, 1, -1), w_down[l],
                      b_down[l].reshape(n_experts, 1, -1), bm)
        x2 = _combine(dest, y, gates, x1, ada3, row(ln_ffn_g[l]), row(ln_ffn_b[l]), seq, alpha)

    return x2.reshape(bsz, seq, d)
```

```python
import functools
import math

import jax
import jax.numpy as jnp
import numpy as np
from jax import lax
from jax.experimental import pallas as pl
from jax.experimental.pallas import tpu as pltpu

CHUNK = 64
ROPE_THETA = 500000.0
EPS = 1e-5
A_HEADS = 8
A_HEAD_DIM = 128
A_ROT_DIM = A_HEAD_DIM // 4
IDX_HEADS = 16
IDX_DIM = 64
IDX_ROT_DIM = IDX_DIM // 4
TOPK_MAX = 256
INDEX_SCALE = (IDX_DIM ** -0.5) * (IDX_HEADS ** -0.5)
B_HEADS = 8
Q_LORA = 512
KV_LORA = 256
QK_NOPE = 128
QK_ROPE = 64
V_HEAD = 128
A_WIDTH = A_HEADS * A_HEAD_DIM
B_WIDTH = B_HEADS * V_HEAD
TOP_K_EXPERTS = 4
SWIGLU_LIMIT = 7.0
SWIGLU_ALPHA = 1.702

LANES = 128
QK_PAD = 256
GROUP_W = 1024
N_GROUPS = 5
VMEM_LIMIT = 56 * 1024 * 1024
NEG_BIG = -1e30
HEAD_GROUP = 4
ROW_DMA_UNROLL = 8
INT_MIN = -2 ** 31

_MXU_DTYPE = jnp.bfloat16


def _cparams(sem, vmem=VMEM_LIMIT):
    return pltpu.CompilerParams(dimension_semantics=sem, vmem_limit_bytes=vmem)


def _tile(n, t):
    t = min(n, t)
    assert n % t == 0, (n, t)
    return t


def _dot(a, b):
    return jnp.dot(a.astype(_MXU_DTYPE), b.astype(_MXU_DTYPE), preferred_element_type=jnp.float32)


def _dot_nt(a, b):
    return lax.dot_general(a.astype(_MXU_DTYPE), b.astype(_MXU_DTYPE), (((1,), (1,)), ((), ())),
                           preferred_element_type=jnp.float32)


def _ln_plain(x):
    mu = jnp.mean(x, axis=-1, keepdims=True)
    d = x - mu
    var = jnp.mean(d * d, axis=-1, keepdims=True)
    return d * lax.rsqrt(var + EPS)


def _rms(x, g):
    return x * lax.rsqrt(jnp.mean(x * x, axis=-1, keepdims=True) + EPS) * g


def _rope128(x, cos, sin, half, group):
    lane = lax.broadcasted_iota(jnp.int32, x.shape, 1) % group
    partner = jnp.where(lane < half, pltpu.roll(x, LANES - half, 1), pltpu.roll(x, half, 1))
    return x * cos + partner * sin


def _ada_kernel(c_ref, w_ref, b_ref, o_ref):
    c = c_ref[...]
    o_ref[...] = _dot(c * jax.nn.sigmoid(c), w_ref[...]) + b_ref[...]


def _ada(c8, w, b):
    d, n = w.shape
    tn = math.gcd(n, 1024)
    return pl.pallas_call(
        _ada_kernel,
        out_shape=jax.ShapeDtypeStruct((8, n), jnp.float32),
        grid=(n // tn,),
        in_specs=[pl.BlockSpec((8, d), lambda j: (0, 0)),
                  pl.BlockSpec((d, tn), lambda j: (0, j)),
                  pl.BlockSpec((1, tn), lambda j: (0, j))],
        out_specs=pl.BlockSpec((8, tn), lambda j: (0, j)),
        compiler_params=_cparams(("arbitrary",)),
        name="ada",
    )(c8, w, b)


def _rope_rows(rot, group):
    half = rot // 2
    inv_freq = ROPE_THETA ** (-jnp.arange(half, dtype=jnp.float32) / half)
    lane = np.arange(LANES) % group
    freq = jnp.where(lane < rot, inv_freq[lane % half], 0.0)
    sign = np.where(lane < half, -1.0, np.where(lane < rot, 1.0, 0.0)).astype(np.float32)
    return freq.reshape(1, LANES).astype(jnp.float32), jnp.asarray(sign).reshape(1, LANES)


def _rope_tables_kernel(pos_ref, fa, sa, fi, si, fm, sm, ca_o, sa_o, ci_o, si_o, cm_o, sm_o):
    pos = pos_ref[...].astype(jnp.float32)
    for f, s, c_o, s_o in ((fa, sa, ca_o, sa_o), (fi, si, ci_o, si_o), (fm, sm, cm_o, sm_o)):
        ang = pos * f[...]
        c_o[...] = jnp.cos(ang)
        s_o[...] = jnp.sin(ang) * s[...]


def _rope_tables(pos_col):
    n = pos_col.shape[0]
    t = _tile(n, 1024)
    rows = (*_rope_rows(A_ROT_DIM, A_HEAD_DIM), *_rope_rows(IDX_ROT_DIM, IDX_DIM), *_rope_rows(QK_ROPE, LANES))
    row_spec = pl.BlockSpec((1, LANES), lambda i: (0, 0))
    tab_spec = pl.BlockSpec((t, LANES), lambda i: (i, 0))
    return pl.pallas_call(
        _rope_tables_kernel,
        out_shape=[jax.ShapeDtypeStruct((n, LANES), jnp.float32)] * 6,
        grid=(n // t,),
        in_specs=[pl.BlockSpec((t, 1), lambda i: (i, 0))] + [row_spec] * 6,
        out_specs=[tab_spec] * 6,
        compiler_params=_cparams(("arbitrary",)),
        name="rope_tables",
    )(pos_col, *rows)


def _proj_kernel(x_ref, sc_ref, sh_ref, w_ref, ca, sa, ci, si, aq_o, ak_o, av_o, iq_o, sm_o, h_scr):
    j = pl.program_id(1)

    @pl.when(j == 0)
    def _():
        h = _ln_plain(x_ref[...]) * (1.0 + sc_ref[0]) + sh_ref[0]
        h_scr[...] = h.astype(h_scr.dtype)

    acc = jnp.dot(h_scr[...], w_ref[...], preferred_element_type=jnp.float32)

    def rope_heads(o_ref, cos_ref, sin_ref, half, group, scale):
        cos, sin = cos_ref[...], sin_ref[...]
        for t in range(GROUP_W // LANES):
            sl = slice(t * LANES, (t + 1) * LANES)
            o_ref[:, sl] = (_rope128(acc[:, sl], cos, sin, half, group) * scale).astype(o_ref.dtype)

    @pl.when(j == 0)
    def _():
        rope_heads(aq_o, ca, sa, A_ROT_DIM // 2, A_HEAD_DIM, A_HEAD_DIM ** -0.5)

    @pl.when(j == 1)
    def _():
        rope_heads(ak_o, ca, sa, A_ROT_DIM // 2, A_HEAD_DIM, 1.0)

    @pl.when(j == 2)
    def _():
        av_o[...] = acc.astype(av_o.dtype)

    @pl.when(j == 3)
    def _():
        rope_heads(iq_o, ci, si, IDX_ROT_DIM // 2, IDX_DIM, 1.0)

    @pl.when(j == 4)
    def _():
        sm_o[...] = acc


def _proj(x2, ada3, w_perm, ca, sa, ci, si, seq):
    n, d = x2.shape
    tm = _tile(seq, 512)
    row = lambda i, j: (i, 0)
    ada_spec = lambda col: pl.BlockSpec((1, 1, d), lambda i, j, col=col: (i * tm // seq, 0, col))
    big = lambda dt: jax.ShapeDtypeStruct((n, GROUP_W), dt)
    return pl.pallas_call(
        _proj_kernel,
        out_shape=[big(_MXU_DTYPE)] * 4 + [big(jnp.float32)],
        grid=(n // tm, N_GROUPS),
        in_specs=[pl.BlockSpec((tm, d), row), ada_spec(1), ada_spec(0),
                  pl.BlockSpec((d, GROUP_W), lambda i, j: (0, j))] + [pl.BlockSpec((tm, LANES), row)] * 4,
        out_specs=[pl.BlockSpec((tm, GROUP_W), row)] * 5,
        scratch_shapes=[pltpu.VMEM((tm, d), _MXU_DTYPE)],
        compiler_params=_cparams(("arbitrary", "arbitrary")),
        name="proj",
    )(x2, ada3, ada3, w_perm, ca, sa, ci, si)


def _mla_prep_kernel(sm_ref, ci, si, cm, sm, qg, kvg, ikg, ikb, wq_ref, wkv_ref,
                     qb_o, kb_o, vb_o, iklo_o, ikhi_o, iw_o):
    small = sm_ref[...]
    qd = small[:, :Q_LORA]
    kvd = small[:, Q_LORA:Q_LORA + KV_LORA]
    ikw = small[:, Q_LORA + KV_LORA:Q_LORA + KV_LORA + LANES]
    kr = small[:, Q_LORA + KV_LORA + LANES:]

    lane = lax.broadcasted_iota(jnp.int32, ikw.shape, 1)
    is_k = lane < IDX_DIM
    ik = jnp.where(is_k, ikw, 0.0)
    mu = jnp.sum(ik, axis=-1, keepdims=True) * (1.0 / IDX_DIM)
    dk = jnp.where(is_k, ikw - mu, 0.0)
    var = jnp.sum(dk * dk, axis=-1, keepdims=True) * (1.0 / IDX_DIM)
    ik = dk * lax.rsqrt(var + EPS) * ikg[...] + ikb[...]
    ik = _rope128(ik, ci[...], si[...], IDX_ROT_DIM // 2, IDX_DIM)
    ik = jnp.where(is_k, ik, 0.0)
    iklo_o[...] = ik.astype(iklo_o.dtype)
    ikhi_o[...] = pltpu.roll(ik, IDX_DIM, 1).astype(ikhi_o.dtype)
    iw_o[...] = pltpu.roll(ikw, LANES - IDX_DIM, 1) * INDEX_SCALE

    cos_m, sin_m = cm[...], sm[...]
    q = _dot(_rms(qd, qg[...]), wq_ref[...])
    scale = (QK_NOPE + QK_ROPE) ** -0.5
    for h in range(B_HEADS):
        lo = slice(h * QK_PAD, h * QK_PAD + LANES)
        hi = slice(h * QK_PAD + LANES, (h + 1) * QK_PAD)
        qb_o[:, lo] = (q[:, lo] * scale).astype(qb_o.dtype)
        qb_o[:, hi] = (_rope128(q[:, hi], cos_m, sin_m, QK_ROPE // 2, LANES) * scale).astype(qb_o.dtype)

    kv = _dot(_rms(kvd, kvg[...]), wkv_ref[...])
    krr = _rope128(kr, cos_m, sin_m, QK_ROPE // 2, LANES).astype(kb_o.dtype)
    for h in range(B_HEADS):
        kb_o[:, h * QK_PAD:h * QK_PAD + LANES] = kv[:, h * QK_NOPE:(h + 1) * QK_NOPE].astype(kb_o.dtype)
        kb_o[:, h * QK_PAD + LANES:(h + 1) * QK_PAD] = krr
    vb_o[...] = kv[:, B_HEADS * QK_NOPE:].astype(vb_o.dtype)


def _mla_prep(small, ci, si, cm, sm, qg, kvg, ikg, ikb, wq, wkv):
    n = small.shape[0]
    tm = _tile(n, 512)
    row = lambda i: (i, 0)
    const = lambda a: pl.BlockSpec(a.shape, lambda i: (0, 0))
    tab = pl.BlockSpec((tm, LANES), row)
    wide = B_HEADS * QK_PAD
    return pl.pallas_call(
        _mla_prep_kernel,
        out_shape=[jax.ShapeDtypeStruct((n, wide), _MXU_DTYPE), jax.ShapeDtypeStruct((n, wide), _MXU_DTYPE),
                   jax.ShapeDtypeStruct((n, B_WIDTH), _MXU_DTYPE),
                   jax.ShapeDtypeStruct((n, LANES), _MXU_DTYPE), jax.ShapeDtypeStruct((n, LANES), _MXU_DTYPE),
                   jax.ShapeDtypeStruct((n, LANES), jnp.float32)],
        grid=(n // tm,),
        in_specs=[pl.BlockSpec((tm, GROUP_W), row), tab, tab, tab, tab,
                  const(qg), const(kvg), const(ikg), const(ikb), const(wq), const(wkv)],
        out_specs=[pl.BlockSpec((tm, wide), row), pl.BlockSpec((tm, wide), row), pl.BlockSpec((tm, B_WIDTH), row),
                   tab, tab, tab],
        compiler_params=_cparams(("arbitrary",)),
        name="mla_prep",
    )(small, ci, si, cm, sm, qg, kvg, ikg, ikb, wq, wkv)


def _indexer_kernel(iq_ref, iklo_ref, ikhi_ref, iw_ref, mask_o, key_scr, *, tq, tk, topk):
    i = pl.program_id(1)
    n_kb_all = mask_o.shape[1]
    n_kb = ((i + 1) * tq + tk - 1) // tk
    q_chunk = (i * tq + lax.broadcasted_iota(jnp.int32, (tq, tk), 0)) // CHUNK
    col = lax.broadcasted_iota(jnp.int32, (tq, tk), 1)
    iw = iw_ref[0]

    def score_block(kb, carry):
        klo = iklo_ref[0, pl.ds(kb * tk, tk), :]
        khi = ikhi_ref[0, pl.ds(kb * tk, tk), :]
        sc = jnp.zeros((tq, tk), jnp.float32)
        for p in range(IDX_HEADS // 2):
            qp = iq_ref[0, :, p * LANES:(p + 1) * LANES]
            for half, kk in ((0, klo), (1, khi)):
                hd = 2 * p + half
                sc = sc + jnp.maximum(_dot_nt(qp, kk), 0.0) * iw[:, hd:hd + 1]
        bits = pltpu.bitcast(sc, jnp.int32)
        key = jnp.where(bits < 0, bits ^ 0x7FFFFFFF, bits)
        visible = (kb * tk + col) // CHUNK <= q_chunk
        key_scr[kb] = jnp.where(visible, key, INT_MIN)
        return carry

    lax.fori_loop(0, n_kb, score_block, 0)

    def bit_pass(it, t_u):
        cand_u = t_u | lax.shift_left(jnp.int32(1), jnp.asarray(31 - it, jnp.int32))
        cand_s = cand_u ^ INT_MIN

        def count_block(kb, acc):
            blk = key_scr[kb]
            for c in range(tk // LANES):
                acc = acc + jnp.where(blk[:, c * LANES:(c + 1) * LANES] >= cand_s, 1.0, 0.0)
            return acc

        acc = lax.fori_loop(0, n_kb, count_block, jnp.zeros((tq, LANES), jnp.float32))
        cnt = jnp.sum(acc, axis=1, keepdims=True)
        return jnp.where(cnt >= topk, cand_u, t_u)

    t_u = lax.fori_loop(0, 32, bit_pass, jnp.zeros((tq, 1), jnp.int32))
    t_s = jnp.maximum(t_u ^ INT_MIN, INT_MIN + 1)

    for kb in range(n_kb_all):
        @pl.when(kb < n_kb)
        def _():
            mask_o[0, kb] = jnp.where(key_scr[kb] >= t_s, 0.0, NEG_BIG).astype(mask_o.dtype)

        @pl.when(kb >= n_kb)
        def _():
            mask_o[0, kb] = jnp.full((tq, tk), NEG_BIG, mask_o.dtype)


def _indexer(iq3, iklo3, ikhi3, iw3, topk):
    b, s, _ = iq3.shape
    tq = _tile(s, 128)
    tk = _tile(s, 512)
    return pl.pallas_call(
        functools.partial(_indexer_kernel, tq=tq, tk=tk, topk=topk),
        out_shape=jax.ShapeDtypeStruct((b, s // tk, s, tk), jnp.bfloat16),
        grid=(b, s // tq),
        in_specs=[pl.BlockSpec((1, tq, GROUP_W), lambda bi, i: (bi, i, 0)),
                  pl.BlockSpec((1, s, LANES), lambda bi, i: (bi, 0, 0)),
                  pl.BlockSpec((1, s, LANES), lambda bi, i: (bi, 0, 0)),
                  pl.BlockSpec((1, tq, LANES), lambda bi, i: (bi, i, 0))],
        out_specs=pl.BlockSpec((1, s // tk, tq, tk), lambda bi, i: (bi, 0, i, 0)),
        scratch_shapes=[pltpu.VMEM((s // tk, tq, tk), jnp.int32)],
        compiler_params=_cparams(("arbitrary", "arbitrary")),
        name="indexer",
    )(iq3, iklo3, ikhi3, iw3)


def _softmax_heads(q_ref, k_ref, v_ref, heads, dqk, dv, n_plain, n_kb, tk, bias_fn, s_scr):
    tq = q_ref.shape[1]
    lane_tiles = [slice(c * LANES, (c + 1) * LANES) for c in range(tk // LANES)]

    def scores(masked):
        def body(kb, mxs):
            rows = pl.ds(pl.multiple_of(kb * tk, tk), tk)
            out = []
            for g, h in enumerate(heads):
                s = _dot_nt(q_ref[0, :, h * dqk:(h + 1) * dqk], k_ref[0, rows, h * dqk:(h + 1) * dqk])
                if masked:
                    s = bias_fn(kb, s)
                s_scr[g, kb] = s
                mx = mxs[g]
                for sl in lane_tiles:
                    mx = jnp.maximum(mx, s[:, sl])
                out.append(mx)
            return tuple(out)
        return body

    mxs = tuple(jnp.full((tq, LANES), NEG_BIG, jnp.float32) for _ in heads)
    mxs = lax.fori_loop(0, n_plain, scores(False), mxs)
    mxs = lax.fori_loop(n_plain, n_kb, scores(True), mxs)
    ms = [jnp.max(mx, axis=1, keepdims=True) for mx in mxs]

    def weigh(kb, carry):
        rows = pl.ds(pl.multiple_of(kb * tk, tk), tk)
        out = []
        for g, h in enumerate(heads):
            l, acc = carry[g]
            p = jnp.exp(s_scr[g, kb] - ms[g])
            for sl in lane_tiles:
                l = l + p[:, sl]
            out.append((l, acc + _dot(p, v_ref[0, rows, h * dv:(h + 1) * dv])))
        return tuple(out)

    zero = (jnp.zeros((tq, LANES), jnp.float32), jnp.zeros((tq, dv), jnp.float32))
    res = lax.fori_loop(0, n_kb, weigh, tuple(zero for _ in heads))
    return [acc / jnp.sum(l, axis=1, keepdims=True) for l, acc in res]


def _attn_a_kernel(q_ref, k_ref, v_ref, bias_ref, g_ref, o_ref, o_scr, s_scr, *, tq, tk):
    i = pl.program_id(1)
    n_kb = ((i + 1) * tq + tk - 1) // tk

    def bias_fn(kb, s):
        return s + bias_ref[0, kb].astype(jnp.float32)

    for h0 in range(0, A_HEADS, HEAD_GROUP):
        heads = list(range(h0, h0 + HEAD_GROUP))
        outs = _softmax_heads(q_ref, k_ref, v_ref, heads, A_HEAD_DIM, A_HEAD_DIM, 0, n_kb, tk, bias_fn, s_scr)
        for h, o in zip(heads, outs):
            o_scr[:, h * A_HEAD_DIM:(h + 1) * A_HEAD_DIM] = o
    o_ref[0] = _rms(o_scr[...], g_ref[...]).astype(o_ref.dtype)


def _attn_a(q3, k3, v3, mask4, g):
    b, s, w = q3.shape
    tk = mask4.shape[3]
    tq = _tile(s, 256)
    full = lambda bi, i: (bi, 0, 0)
    return pl.pallas_call(
        functools.partial(_attn_a_kernel, tq=tq, tk=tk),
        out_shape=jax.ShapeDtypeStruct((b, s, w), _MXU_DTYPE),
        grid=(b, s // tq),
        in_specs=[pl.BlockSpec((1, tq, w), lambda bi, i: (bi, i, 0)),
                  pl.BlockSpec((1, s, w), full, pipeline_mode=pl.Buffered(1)),
                  pl.BlockSpec((1, s, w), full, pipeline_mode=pl.Buffered(1)),
                  pl.BlockSpec((1, s // tk, tq, tk), lambda bi, i: (bi, 0, i, 0)),
                  pl.BlockSpec((1, w), lambda bi, i: (0, 0))],
        out_specs=pl.BlockSpec((1, tq, w), lambda bi, i: (bi, i, 0)),
        scratch_shapes=[pltpu.VMEM((tq, w), jnp.float32), pltpu.VMEM((HEAD_GROUP, s // tk, tq, tk), jnp.float32)],
        compiler_params=_cparams(("arbitrary", "arbitrary")),
        name="attn_a",
    )(q3, k3, v3, mask4, g)


def _attn_b_kernel(q_ref, k_ref, v_ref, g_ref, o_ref, o_scr, s_scr, *, tq, tk):
    i = pl.program_id(1)
    n_kb = ((i + 1) * tq + tk - 1) // tk
    n_plain = (i * tq) // tk
    q_chunk = (i * tq + lax.broadcasted_iota(jnp.int32, (tq, tk), 0)) // CHUNK
    col = lax.broadcasted_iota(jnp.int32, (tq, tk), 1)

    def bias_fn(kb, s):
        return jnp.where((kb * tk + col) // CHUNK <= q_chunk, s, NEG_BIG)

    for h0 in range(0, B_HEADS, HEAD_GROUP):
        heads = list(range(h0, h0 + HEAD_GROUP))
        outs = _softmax_heads(q_ref, k_ref, v_ref, heads, QK_PAD, V_HEAD, n_plain, n_kb, tk, bias_fn, s_scr)
        for h, o in zip(heads, outs):
            o_scr[:, h * V_HEAD:(h + 1) * V_HEAD] = o
    o_ref[0] = _rms(o_scr[...], g_ref[...]).astype(o_ref.dtype)


def _attn_b(q3, k3, v3, g):
    b, s, wq = q3.shape
    wv = v3.shape[2]
    tq = _tile(s, 256)
    tk = _tile(s, 512)
    full = lambda bi, i: (bi, 0, 0)
    return pl.pallas_call(
        functools.partial(_attn_b_kernel, tq=tq, tk=tk),
        out_shape=jax.ShapeDtypeStruct((b, s, wv), _MXU_DTYPE),
        grid=(b, s // tq),
        in_specs=[pl.BlockSpec((1, tq, wq), lambda bi, i: (bi, i, 0)),
                  pl.BlockSpec((1, s, wq), full, pipeline_mode=pl.Buffered(1)),
                  pl.BlockSpec((1, s, wv), full, pipeline_mode=pl.Buffered(1)),
                  pl.BlockSpec((1, wv), lambda bi, i: (0, 0))],
        out_specs=pl.BlockSpec((1, tq, wv), lambda bi, i: (bi, i, 0)),
        scratch_shapes=[pltpu.VMEM((tq, wv), jnp.float32), pltpu.VMEM((HEAD_GROUP, s // tk, tq, tk), jnp.float32)],
        compiler_params=_cparams(("arbitrary", "arbitrary")),
        name="attn_b",
    )(q3, k3, v3, g)


def _outproj_kernel(ma_ref, mb_ref, w_ref, x_ref, g1_ref, sc2_ref, sh2_ref, lg_ref, lb_ref, wr_ref, br_ref,
                    x1_o, h2_o, lgt_o, *, alpha, n_experts):
    mix = (jnp.dot(ma_ref[...], w_ref[:A_WIDTH, :], preferred_element_type=jnp.float32)
           + jnp.dot(mb_ref[...], w_ref[A_WIDTH:, :], preferred_element_type=jnp.float32))
    x1 = _ln_plain(alpha * x_ref[...] + g1_ref[0] * mix) * lg_ref[...] + lb_ref[...]
    x1_o[...] = x1
    h2 = _ln_plain(x1) * (1.0 + sc2_ref[0]) + sh2_ref[0]
    h2_o[...] = h2
    logits = jnp.dot(h2, wr_ref[...], preferred_element_type=jnp.float32,
                     precision=lax.Precision.HIGHEST) + br_ref[...]
    lane = lax.broadcasted_iota(jnp.int32, logits.shape, 1)
    lgt_o[...] = jnp.where(lane < n_experts, logits, -jnp.inf)


def _outproj(ma, mb, w_out, x2, ada3, lg, lb, wr, br, seq, alpha, n_experts):
    n, d = x2.shape
    tm = _tile(seq, 256)
    row = lambda i: (i, 0)
    const = lambda a: pl.BlockSpec(a.shape, lambda i: (0,) * a.ndim)
    ada_spec = lambda col: pl.BlockSpec((1, 1, d), lambda i, col=col: (i * tm // seq, 0, col))
    return pl.pallas_call(
        functools.partial(_outproj_kernel, alpha=alpha, n_experts=n_experts),
        out_shape=[jax.ShapeDtypeStruct((n, d), jnp.float32),
                   jax.ShapeDtypeStruct((n, d), jnp.float32),
                   jax.ShapeDtypeStruct((n, LANES), jnp.float32)],
        grid=(n // tm,),
        in_specs=[pl.BlockSpec((tm, A_WIDTH), row), pl.BlockSpec((tm, B_WIDTH), row),
                  pl.BlockSpec(w_out.shape, lambda i: (0, 0), pipeline_mode=pl.Buffered(1)),
                  pl.BlockSpec((tm, d), row), ada_spec(2), ada_spec(4), ada_spec(3),
                  const(lg), const(lb), const(wr), const(br)],
        out_specs=[pl.BlockSpec((tm, d), row), pl.BlockSpec((tm, d), row), pl.BlockSpec((tm, LANES), row)],
        compiler_params=_cparams(("arbitrary",)),
        name="outproj",
    )(ma, mb, w_out, x2, ada3, ada3, ada3, lg, lb, wr, br)


def _route_kernel(lgt_ref, top_o, gate_o, rank_o, cnt_o, carry):
    tb = lgt_ref.shape[0]

    @pl.when(pl.program_id(0) == 0)
    def _():
        carry[...] = jnp.zeros_like(carry)

    lane = lax.broadcasted_iota(jnp.int32, (tb, LANES), 1)
    lane_f = lane.astype(jnp.float32)
    work = lgt_ref[...]
    vals, idxs, hots = [], [], []
    for _ in range(TOP_K_EXPERTS):
        m = jnp.max(work, axis=1, keepdims=True)
        idx = jnp.min(jnp.where(work == m, lane_f, float(LANES)), axis=1, keepdims=True)
        hot = lane_f == idx
        vals.append(m)
        idxs.append(idx)
        hots.append(hot)
        work = jnp.where(hot, -jnp.inf, work)

    exps = [jnp.exp(v - vals[0]) for v in vals]
    denom = exps[0] + exps[1] + exps[2] + exps[3]

    member = jnp.zeros((tb, LANES), jnp.float32)
    for hot in hots:
        member = member + jnp.where(hot, 1.0, 0.0)
    r = lax.broadcasted_iota(jnp.int32, (tb, tb), 0)
    c = lax.broadcasted_iota(jnp.int32, (tb, tb), 1)
    before = jnp.where(c < r, 1.0, 0.0).astype(jnp.bfloat16)
    prefix = jnp.dot(before, member.astype(jnp.bfloat16), preferred_element_type=jnp.float32) + carry[...]

    top = jnp.zeros((tb, LANES), jnp.int32)
    gate = jnp.zeros((tb, LANES), jnp.float32)
    rank = jnp.zeros((tb, LANES), jnp.int32)
    for k in range(TOP_K_EXPERTS):
        rk = jnp.sum(jnp.where(hots[k], prefix, 0.0), axis=1, keepdims=True)
        top = jnp.where(lane == k, idxs[k].astype(jnp.int32), top)
        gate = jnp.where(lane == k, exps[k] / denom, gate)
        rank = jnp.where(lane == k, rk.astype(jnp.int32), rank)
    top_o[...] = top
    gate_o[...] = gate
    rank_o[...] = rank
    carry[...] = carry[...] + jnp.sum(member, axis=0, keepdims=True)
    cnt_o[...] = jnp.broadcast_to(carry[...], cnt_o.shape).astype(jnp.int32)


def _route(logits):
    n = logits.shape[0]
    tb = _tile(n, 512)
    row = pl.BlockSpec((tb, LANES), lambda i: (i, 0))
    return pl.pallas_call(
        _route_kernel,
        out_shape=[jax.ShapeDtypeStruct((n, LANES), jnp.int32), jax.ShapeDtypeStruct((n, LANES), jnp.float32),
                   jax.ShapeDtypeStruct((n, LANES), jnp.int32), jax.ShapeDtypeStruct((8, LANES), jnp.int32)],
        grid=(n // tb,),
        in_specs=[row],
        out_specs=[row, row, row, pl.BlockSpec((8, LANES), lambda i: (0, 0))],
        scratch_shapes=[pltpu.VMEM((1, LANES), jnp.float32)],
        compiler_params=_cparams(("arbitrary",)),
        name="route",
    )(logits)


def _row_copy(src, s_row, dst, d_row, sem):
    return pltpu.make_async_copy(src.at[pl.ds(s_row, 1), :], dst.at[pl.ds(d_row, 1), :], sem)


def _dispatch_kernel(dest_ref, h_ref, xs_in, xs_hbm, sem, *, td):
    del xs_in
    base = pl.program_id(0) * td

    def issue(t, c):
        for k in range(TOP_K_EXPERTS):
            _row_copy(h_ref, t, xs_hbm, dest_ref[(base + t) * TOP_K_EXPERTS + k], sem).start()
        return c

    lax.fori_loop(0, td, issue, 0, unroll=ROW_DMA_UNROLL)
    landed = xs_hbm.at[pl.ds(0, td * TOP_K_EXPERTS), :]
    pltpu.make_async_copy(landed, landed, sem).wait()


def _dispatch(dest_flat, h2, rows):
    n, d = h2.shape
    td = _tile(n, 256)
    xs0 = jnp.zeros((rows, d), h2.dtype)
    return pl.pallas_call(
        functools.partial(_dispatch_kernel, td=td),
        out_shape=jax.ShapeDtypeStruct(xs0.shape, xs0.dtype),
        grid_spec=pltpu.PrefetchScalarGridSpec(
            num_scalar_prefetch=1, grid=(n // td,),
            in_specs=[pl.BlockSpec((td, d), lambda i, dr: (i, 0)), pl.BlockSpec(memory_space=pl.ANY)],
            out_specs=pl.BlockSpec(memory_space=pl.ANY),
            scratch_shapes=[pltpu.SemaphoreType.DMA(())]),
        input_output_aliases={2: 0},
        compiler_params=pltpu.CompilerParams(dimension_semantics=("arbitrary",), has_side_effects=True,
                                             disable_bounds_checks=True),
        name="dispatch",
    )(dest_flat, h2, xs0)


def _swiglu_compact(gu):
    tf2 = gu.shape[1]
    nxt = pltpu.roll(gu, tf2 - 1, 1)
    g = jnp.minimum(gu, SWIGLU_LIMIT)
    u = jnp.clip(nxt, -SWIGLU_LIMIT, SWIGLU_LIMIT)
    act = ((u + 1.0) * (g * jax.nn.sigmoid(SWIGLU_ALPHA * g))).astype(_MXU_DTYPE)
    r = lax.broadcasted_iota(jnp.int32, (2 * LANES, LANES), 0)
    c = lax.broadcasted_iota(jnp.int32, (2 * LANES, LANES), 1)
    pick = jnp.where(r == 2 * c, 1.0, 0.0).astype(_MXU_DTYPE)
    parts = [jnp.dot(act[:, t * 2 * LANES:(t + 1) * 2 * LANES], pick, preferred_element_type=jnp.float32)
             for t in range(tf2 // (2 * LANES))]
    return jnp.concatenate(parts, axis=1).astype(_MXU_DTYPE)


def _moe_gemm_kernel(sbe_ref, row0_ref, nsub_ref, xs_hbm, wgu_ref, bgu_ref, wd_ref, bd_ref, y_hbm,
                     xstage, xb, act, wb, wdb, ystage, gu_scr, sem_x, sem_y, *, sub, n_f):
    del sbe_ref
    sb = pl.program_id(0)
    t = pl.program_id(1)
    nsub = nsub_ref[sb]
    row0 = row0_ref[sb]
    tf = wd_ref.shape[1]

    def x_copy(c, slot):
        rows = pl.ds(pl.multiple_of(row0 + c * sub, sub), sub)
        return pltpu.make_async_copy(xs_hbm.at[rows, :], xstage.at[slot], sem_x.at[slot])

    def y_copy(s, slot):
        rows = pl.ds(pl.multiple_of(row0 + s * sub, sub), sub)
        return pltpu.make_async_copy(ystage.at[slot], y_hbm.at[rows, :], sem_y.at[slot])

    @pl.when(jnp.logical_and(sb == 0, t == 0))
    def _():
        xb[...] = jnp.zeros_like(xb)

    @pl.when(jnp.logical_and(nsub > 0, t < n_f))
    def _():
        wb[...] = wgu_ref[0].astype(wb.dtype)
        wdb[pl.ds(pl.multiple_of(t * tf, tf), tf), :] = wd_ref[0].astype(wdb.dtype)
        bias = bgu_ref[0]

        def gate_up(x):
            return jnp.dot(x, wb[...], preferred_element_type=jnp.float32) + bias

        @pl.when(t == 0)
        def _():
            x_copy(0, 0).start()

            def first(c, carry):
                slot = c % 2

                @pl.when(c + 1 < nsub)
                def _():
                    x_copy(c + 1, 1 - slot).start()

                x_copy(c, slot).wait()
                rows = pl.ds(pl.multiple_of(c * sub, sub), sub)
                xv = xstage[slot].astype(xb.dtype)
                xb[rows, :] = xv
                act[0, rows, :] = _swiglu_compact(gate_up(xv))
                return carry

            lax.fori_loop(0, nsub, first, 0)

        @pl.when(t > 0)
        def _():
            n_trips = (nsub + 1) // 2
            trip_rows = lambda p: pl.ds(pl.multiple_of(p * 2 * sub, 2 * sub), 2 * sub)
            gu_scr[...] = gate_up(xb[trip_rows(0), :])

            def up(p, carry):
                prev = gu_scr[...]
                gu_scr[...] = gate_up(xb[trip_rows(p), :])
                act[t, trip_rows(p - 1), :] = _swiglu_compact(prev)
                return carry

            lax.fori_loop(1, n_trips, up, 0)
            act[t, trip_rows(n_trips - 1), :] = _swiglu_compact(gu_scr[...])

    @pl.when(jnp.logical_and(nsub > 0, t == n_f))
    def _():
        bias = bd_ref[0]

        def down(s, carry):
            slot = s % 2
            rows = pl.ds(pl.multiple_of(s * sub, sub), sub)
            a = jnp.concatenate([act[f, rows, :] for f in range(n_f)], axis=1)
            yt = jnp.dot(a, wdb[...], preferred_element_type=jnp.float32) + bias

            @pl.when(s >= 2)
            def _():
                y_copy(s - 2, slot).wait()

            ystage[slot] = yt
            y_copy(s, slot).start()
            return carry

        lax.fori_loop(0, nsub, down, 0)

        @pl.when(nsub >= 2)
        def _():
            y_copy(nsub - 2, nsub % 2).wait()

        y_copy(nsub - 1, (nsub - 1) % 2).wait()


def _moe_gemm(sb_e, sb_row0, sb_nsub, xs, wgu, bgu3, wd, bd3, sub, max_sub):
    e, d, ff2 = wgu.shape
    ff = ff2 // 2
    tf = _tile(ff, 256)
    n_f = ff // tf
    n_sb = sb_e.shape[0]
    rm = (max_sub + 1) // 2 * 2 * sub
    f_idx = lambda sb, t, ns: jnp.where(ns[sb] > 0, jnp.minimum(t, n_f - 1), n_f - 1)
    return pl.pallas_call(
        functools.partial(_moe_gemm_kernel, sub=sub, n_f=n_f),
        out_shape=jax.ShapeDtypeStruct(xs.shape, jnp.float32),
        grid_spec=pltpu.PrefetchScalarGridSpec(
            num_scalar_prefetch=3, grid=(n_sb, n_f + 1),
            in_specs=[pl.BlockSpec(memory_space=pl.ANY),
                      pl.BlockSpec((1, d, 2 * tf), lambda sb, t, se, r0, ns: (se[sb], 0, f_idx(sb, t, ns))),
                      pl.BlockSpec((1, 1, 2 * tf), lambda sb, t, se, r0, ns: (se[sb], 0, f_idx(sb, t, ns))),
                      pl.BlockSpec((1, tf, d), lambda sb, t, se, r0, ns: (se[sb], f_idx(sb, t, ns), 0)),
                      pl.BlockSpec((1, 1, d), lambda sb, t, se, r0, ns: (se[sb], 0, 0))],
            out_specs=pl.BlockSpec(memory_space=pl.ANY),
            scratch_shapes=[pltpu.VMEM((2, sub, d), jnp.float32),
                            pltpu.VMEM((rm, d), _MXU_DTYPE),
                            pltpu.VMEM((n_f, rm, tf), _MXU_DTYPE),
                            pltpu.VMEM((d, 2 * tf), _MXU_DTYPE),
                            pltpu.VMEM((ff, d), _MXU_DTYPE),
                            pltpu.VMEM((2, sub, d), jnp.float32),
                            pltpu.VMEM((2 * sub, 2 * tf), jnp.float32),
                            pltpu.SemaphoreType.DMA((2,)), pltpu.SemaphoreType.DMA((2,))]),
        input_output_aliases={3: 0},
        compiler_params=_cparams(("arbitrary", "arbitrary")),
        name="moe_gemm",
    )(sb_e, sb_row0, sb_nsub, xs, wgu, bgu3, wd, bd3)


def _combine_kernel(dest_ref, y_hbm, gate_ref, x1_ref, g2_ref, lg_ref, lb_ref, o_ref, buf, sem, *, tc, alpha):
    base = pl.program_id(0) * tc

    def issue(t, c):
        for k in range(TOP_K_EXPERTS):
            _row_copy(y_hbm, dest_ref[(base + t) * TOP_K_EXPERTS + k], buf, k * tc + t, sem).start()
        return c

    lax.fori_loop(0, tc, issue, 0, unroll=ROW_DMA_UNROLL)
    pltpu.make_async_copy(y_hbm.at[pl.ds(0, tc * TOP_K_EXPERTS), :], buf, sem).wait()

    gate = gate_ref[...]
    ffn = buf[0:tc, :] * gate[:, 0:1]
    for k in range(1, TOP_K_EXPERTS):
        ffn = ffn + buf[k * tc:(k + 1) * tc, :] * gate[:, k:k + 1]
    o_ref[...] = _ln_plain(alpha * x1_ref[...] + g2_ref[0] * ffn) * lg_ref[...] + lb_ref[...]


def _combine(dest_flat, y, gates, x1, ada3, lg, lb, seq, alpha):
    n, d = x1.shape
    tc = _tile(seq, 256)
    row = lambda i, dr: (i, 0)
    return pl.pallas_call(
        functools.partial(_combine_kernel, tc=tc, alpha=alpha),
        out_shape=jax.ShapeDtypeStruct((n, d), jnp.float32),
        grid_spec=pltpu.PrefetchScalarGridSpec(
            num_scalar_prefetch=1, grid=(n // tc,),
            in_specs=[pl.BlockSpec(memory_space=pl.ANY),
                      pl.BlockSpec((tc, LANES), row), pl.BlockSpec((tc, d), row),
                      pl.BlockSpec((1, 1, d), lambda i, dr: (i * tc // seq, 0, 5)),
                      pl.BlockSpec((1, d), lambda i, dr: (0, 0)), pl.BlockSpec((1, d), lambda i, dr: (0, 0))],
            out_specs=pl.BlockSpec((tc, d), row),
            scratch_shapes=[pltpu.VMEM((TOP_K_EXPERTS * tc, d), jnp.float32),
                            pltpu.SemaphoreType.DMA(())]),
        compiler_params=pltpu.CompilerParams(dimension_semantics=("arbitrary",), vmem_limit_bytes=VMEM_LIMIT,
                                             disable_bounds_checks=True),
        name="combine",
    )(dest_flat, y, gates, x1, ada3, lg, lb)


def _pad_cols(a, width):
    return jnp.pad(a, ((0, 0), (0, width - a.shape[1])))


def _perm_w_in(w_in):
    o = np.cumsum([0, A_WIDTH, A_WIDTH, A_WIDTH, IDX_HEADS * IDX_DIM, IDX_DIM, IDX_HEADS, Q_LORA, KV_LORA, QK_ROPE])
    seg = lambda i: w_in[:, o[i]:o[i + 1]]
    ikw = _pad_cols(jnp.concatenate([seg(4), seg(5)], axis=1), LANES)
    kr = _pad_cols(seg(8), LANES)
    return jnp.concatenate([seg(0), seg(1), seg(2), seg(3), seg(6), seg(7), ikw, kr], axis=1)


def _perm_w_q_up(w):
    w = w.reshape(Q_LORA, B_HEADS, QK_NOPE + QK_ROPE)
    w = jnp.pad(w, ((0, 0), (0, 0), (0, QK_PAD - QK_NOPE - QK_ROPE)))
    return w.reshape(Q_LORA, B_HEADS * QK_PAD)


def _perm_w_kv_up(w):
    w = w.reshape(KV_LORA, B_HEADS, QK_NOPE + V_HEAD)
    return jnp.concatenate([w[:, :, :QK_NOPE].reshape(KV_LORA, -1), w[:, :, QK_NOPE:].reshape(KV_LORA, -1)], axis=1)


MOE_MAX_SUB = 9


def _moe_sub_rows(n_tokens):
    return 256 if n_tokens >= 4096 else 64


def _super_blocks(cnt, sub, n_assign):
    n_experts = cnt.shape[0]
    q = (cnt + sub - 1) // sub
    pad_ends = jnp.cumsum(q * sub)
    pad_starts = pad_ends - q * sub
    nsb_e = (q + MOE_MAX_SUB - 1) // MOE_MAX_SUB
    sb_end = jnp.cumsum(nsb_e)
    sb_start = sb_end - nsb_e
    n_sb = (n_assign // sub + n_experts + MOE_MAX_SUB - 1) // MOE_MAX_SUB + n_experts
    idx = jnp.arange(n_sb, dtype=jnp.int32)
    valid = idx < sb_end[-1]
    e_of = jnp.minimum(jnp.sum((sb_end[None, :] <= idx[:, None]).astype(jnp.int32), axis=1), n_experts - 1)
    j = idx - sb_start[e_of]
    parts = jnp.maximum(nsb_e[e_of], 1)
    base, rem = q[e_of] // parts, q[e_of] % parts
    nsub = jnp.where(valid, base + (j < rem).astype(jnp.int32), 0)
    row0 = jnp.where(valid, pad_starts[e_of] + (j * base + jnp.minimum(j, rem)) * sub, 0)
    e_last = jnp.max(jnp.where(valid, e_of, 0))
    sb_e = jnp.where(valid, e_of, e_last)
    i32 = lambda a: a.astype(jnp.int32)
    return pad_starts, i32(sb_e), i32(row0), i32(nsub)


def kernel(x, c, positions, w_ada, b_ada, w_in, idx_k_norm_g, idx_k_norm_b, q_norm_g, w_q_up, kv_norm_g, w_kv_up,
           out_norm_a_g, out_norm_b_g, w_out, ln_mix_g, ln_mix_b, w_router, b_router, w_gate_up, b_gate_up,
           w_down, b_down, ln_ffn_g, ln_ffn_b):
    bsz, seq, d = x.shape
    depth = w_ada.shape[0]
    n_experts = w_router.shape[2]
    n = bsz * seq
    alpha = (2 * depth) ** 0.25
    topk = min(TOPK_MAX, seq // 4)
    sub = _moe_sub_rows(n)
    n_assign = n * TOP_K_EXPERTS
    rows = (n_assign // sub + n_experts) * sub

    ca, sa, ci, si, cm, sm = _rope_tables(positions.reshape(n, 1))
    c8 = jnp.pad(c, ((0, 8 - bsz), (0, 0)))
    x2 = x.reshape(n, d)
    row = lambda a: a.reshape(1, -1)

    for l in range(depth):
        ada3 = _ada(c8, w_ada[l], row(b_ada[l]))[:bsz].reshape(bsz, 1, 6 * d)

        aq, ak, av, iq, small = _proj(x2, ada3, _perm_w_in(w_in[l]).astype(_MXU_DTYPE), ca, sa, ci, si, seq)
        qb, kb, vb, iklo, ikhi, iw = _mla_prep(
            small, ci, si, cm, sm, row(q_norm_g[l]), row(kv_norm_g[l]),
            _pad_cols(row(idx_k_norm_g[l]), LANES), _pad_cols(row(idx_k_norm_b[l]), LANES),
            _perm_w_q_up(w_q_up[l]).astype(_MXU_DTYPE), _perm_w_kv_up(w_kv_up[l]).astype(_MXU_DTYPE))
        b3 = lambda a: a.reshape(bsz, seq, a.shape[-1])
        mask = _indexer(b3(iq), b3(iklo), b3(ikhi), b3(iw), topk)
        out_a = _attn_a(b3(aq), b3(ak), b3(av), mask, row(out_norm_a_g[l]))
        out_b = _attn_b(b3(qb), b3(kb), b3(vb), row(out_norm_b_g[l]))

        x1, h2, logits = _outproj(
            out_a.reshape(n, A_WIDTH), out_b.reshape(n, B_WIDTH), w_out[l].astype(_MXU_DTYPE), x2, ada3,
            row(ln_mix_g[l]), row(ln_mix_b[l]), _pad_cols(w_router[l], LANES), _pad_cols(row(b_router[l]), LANES),
            seq, alpha, n_experts)
        top, gates, rank, counts = _route(logits)

        pad_starts, sb_e, sb_row0, sb_nsub = _super_blocks(counts[0, :n_experts], sub, n_assign)
        dest = (pad_starts[top[:, :TOP_K_EXPERTS]] + rank[:, :TOP_K_EXPERTS]).reshape(-1).astype(jnp.int32)

        xs = _dispatch(dest, h2, rows)
        y = _moe_gemm(sb_e, sb_row0, sb_nsub, xs, w_gate_up[l], b_gate_up[l].reshape(n_experts, 1, -1), w_down[l],
                      b_down[l].reshape(n_experts, 1, -1), sub, MOE_MAX_SUB)
        x2 = _combine(dest, y, gates, x1, ada3, row(ln_ffn_g[l]), row(ln_ffn_b[l]), seq, alpha)

    return x2.reshape(bsz, seq, d)
```

```python
import functools
import math

import jax
import jax.numpy as jnp
import numpy as np
from jax import lax
from jax.experimental import pallas as pl
from jax.experimental.pallas import tpu as pltpu

CHUNK = 64
ROPE_THETA = 500000.0
EPS = 1e-5
A_HEADS = 8
A_HEAD_DIM = 128
A_ROT_DIM = A_HEAD_DIM // 4
IDX_HEADS = 16
IDX_DIM = 64
IDX_ROT_DIM = IDX_DIM // 4
TOPK_MAX = 256
INDEX_SCALE = (IDX_DIM ** -0.5) * (IDX_HEADS ** -0.5)
B_HEADS = 8
Q_LORA = 512
KV_LORA = 256
QK_NOPE = 128
QK_ROPE = 64
V_HEAD = 128
A_WIDTH = A_HEADS * A_HEAD_DIM
B_WIDTH = B_HEADS * V_HEAD
TOP_K_EXPERTS = 4
SWIGLU_LIMIT = 7.0
SWIGLU_ALPHA = 1.702

LANES = 128
QK_PAD = 256
GROUP_W = 1024
VMEM_LIMIT = 56 * 1024 * 1024
NEG_BIG = -1e30
HEAD_GROUP = 4
ROW_DMA_UNROLL = 8
INT_MIN = -2 ** 31

_MXU_DTYPE = jnp.bfloat16


def _cparams(sem, vmem=VMEM_LIMIT):
    return pltpu.CompilerParams(dimension_semantics=sem, vmem_limit_bytes=vmem)


def _tile(n, t):
    t = min(n, t)
    assert n % t == 0, (n, t)
    return t


def _dot(a, b):
    return jnp.dot(a.astype(_MXU_DTYPE), b.astype(_MXU_DTYPE), preferred_element_type=jnp.float32)


def _dot_nt(a, b):
    return lax.dot_general(a.astype(_MXU_DTYPE), b.astype(_MXU_DTYPE), (((1,), (1,)), ((), ())),
                           preferred_element_type=jnp.float32)


def _ln_plain(x):
    mu = jnp.mean(x, axis=-1, keepdims=True)
    d = x - mu
    var = jnp.mean(d * d, axis=-1, keepdims=True)
    return d * lax.rsqrt(var + EPS)


def _rms(x, g):
    return x * lax.rsqrt(jnp.mean(x * x, axis=-1, keepdims=True) + EPS) * g


def _rope128(x, cos, sin, half, group):
    lane = lax.broadcasted_iota(jnp.int32, x.shape, 1) % group
    partner = jnp.where(lane < half, pltpu.roll(x, LANES - half, 1), pltpu.roll(x, half, 1))
    return x * cos + partner * sin


def _ada_kernel(c_ref, w_ref, b_ref, o_ref):
    c = c_ref[...]
    o_ref[...] = _dot(c * jax.nn.sigmoid(c), w_ref[...]) + b_ref[...]


def _ada(c8, w, b):
    d, n = w.shape
    tn = math.gcd(n, 1024)
    return pl.pallas_call(
        _ada_kernel,
        out_shape=jax.ShapeDtypeStruct((8, n), jnp.float32),
        grid=(n // tn,),
        in_specs=[pl.BlockSpec((8, d), lambda j: (0, 0)),
                  pl.BlockSpec((d, tn), lambda j: (0, j)),
                  pl.BlockSpec((1, tn), lambda j: (0, j))],
        out_specs=pl.BlockSpec((8, tn), lambda j: (0, j)),
        compiler_params=_cparams(("arbitrary",)),
        name="ada",
    )(c8, w, b)


def _rope_rows(rot, group):
    half = rot // 2
    inv_freq = ROPE_THETA ** (-jnp.arange(half, dtype=jnp.float32) / half)
    lane = np.arange(LANES) % group
    freq = jnp.where(lane < rot, inv_freq[lane % half], 0.0)
    sign = np.where(lane < half, -1.0, np.where(lane < rot, 1.0, 0.0)).astype(np.float32)
    return freq.reshape(1, LANES).astype(jnp.float32), jnp.asarray(sign).reshape(1, LANES)


def _rope_tables_kernel(pos_ref, fa, sa, fi, si, fm, sm, ca_o, sa_o, ci_o, si_o, cm_o, sm_o):
    pos = pos_ref[...].astype(jnp.float32)
    for f, s, c_o, s_o in ((fa, sa, ca_o, sa_o), (fi, si, ci_o, si_o), (fm, sm, cm_o, sm_o)):
        ang = pos * f[...]
        c_o[...] = jnp.cos(ang)
        s_o[...] = jnp.sin(ang) * s[...]


def _rope_tables(pos_col):
    n = pos_col.shape[0]
    t = _tile(n, 1024)
    rows = (*_rope_rows(A_ROT_DIM, A_HEAD_DIM), *_rope_rows(IDX_ROT_DIM, IDX_DIM), *_rope_rows(QK_ROPE, LANES))
    row_spec = pl.BlockSpec((1, LANES), lambda i: (0, 0))
    tab_spec = pl.BlockSpec((t, LANES), lambda i: (i, 0))
    return pl.pallas_call(
        _rope_tables_kernel,
        out_shape=[jax.ShapeDtypeStruct((n, LANES), jnp.float32)] * 6,
        grid=(n // t,),
        in_specs=[pl.BlockSpec((t, 1), lambda i: (i, 0))] + [row_spec] * 6,
        out_specs=[tab_spec] * 6,
        compiler_params=_cparams(("arbitrary",)),
        name="rope_tables",
    )(pos_col, *rows)


def _proj_kernel(x_ref, sc_ref, sh_ref, w_ref, ca, sa, ci, si, aq_o, ak_o, av_o, iq_o, sm_o):
    h = (_ln_plain(x_ref[...]) * (1.0 + sc_ref[0]) + sh_ref[0]).astype(w_ref.dtype)

    def group(j):
        return jnp.dot(h, w_ref[:, j * GROUP_W:(j + 1) * GROUP_W], preferred_element_type=jnp.float32)

    def rope_heads(acc, o_ref, cos_ref, sin_ref, half, group_w, scale):
        cos, sin = cos_ref[...], sin_ref[...]
        for t in range(GROUP_W // LANES):
            sl = slice(t * LANES, (t + 1) * LANES)
            o_ref[:, sl] = (_rope128(acc[:, sl], cos, sin, half, group_w) * scale).astype(o_ref.dtype)

    rope_heads(group(0), aq_o, ca, sa, A_ROT_DIM // 2, A_HEAD_DIM, A_HEAD_DIM ** -0.5)
    rope_heads(group(1), ak_o, ca, sa, A_ROT_DIM // 2, A_HEAD_DIM, 1.0)
    av_o[...] = group(2).astype(av_o.dtype)
    rope_heads(group(3), iq_o, ci, si, IDX_ROT_DIM // 2, IDX_DIM, 1.0)
    sm_o[...] = group(4)


def _proj(x2, ada3, w_perm, ca, sa, ci, si, seq):
    n, d = x2.shape
    tm = _tile(seq, 512)
    row = lambda i: (i, 0)
    ada_spec = lambda col: pl.BlockSpec((1, 1, d), lambda i, col=col: (i * tm // seq, 0, col))
    big = lambda dt: jax.ShapeDtypeStruct((n, GROUP_W), dt)
    return pl.pallas_call(
        _proj_kernel,
        out_shape=[big(_MXU_DTYPE)] * 4 + [big(jnp.float32)],
        grid=(n // tm,),
        in_specs=[pl.BlockSpec((tm, d), row), ada_spec(1), ada_spec(0),
                  pl.BlockSpec(w_perm.shape, lambda i: (0, 0), pipeline_mode=pl.Buffered(1))]
                 + [pl.BlockSpec((tm, LANES), row)] * 4,
        out_specs=[pl.BlockSpec((tm, GROUP_W), row)] * 5,
        compiler_params=_cparams(("arbitrary",)),
        name="proj",
    )(x2, ada3, ada3, w_perm, ca, sa, ci, si)


def _mla_prep_kernel(sm_ref, ci, si, cm, sm, qg, kvg, ikg, ikb, wq_ref, wkv_ref,
                     qb_o, kb_o, vb_o, iklo_o, ikhi_o, iw_o):
    small = sm_ref[...]
    qd = small[:, :Q_LORA]
    kvd = small[:, Q_LORA:Q_LORA + KV_LORA]
    ikw = small[:, Q_LORA + KV_LORA:Q_LORA + KV_LORA + LANES]
    kr = small[:, Q_LORA + KV_LORA + LANES:]

    lane = lax.broadcasted_iota(jnp.int32, ikw.shape, 1)
    is_k = lane < IDX_DIM
    ik = jnp.where(is_k, ikw, 0.0)
    mu = jnp.sum(ik, axis=-1, keepdims=True) * (1.0 / IDX_DIM)
    dk = jnp.where(is_k, ikw - mu, 0.0)
    var = jnp.sum(dk * dk, axis=-1, keepdims=True) * (1.0 / IDX_DIM)
    ik = dk * lax.rsqrt(var + EPS) * ikg[...] + ikb[...]
    ik = _rope128(ik, ci[...], si[...], IDX_ROT_DIM // 2, IDX_DIM)
    ik = jnp.where(is_k, ik, 0.0)
    iklo_o[...] = ik.astype(iklo_o.dtype)
    ikhi_o[...] = pltpu.roll(ik, IDX_DIM, 1).astype(ikhi_o.dtype)
    iw_o[...] = pltpu.roll(ikw, LANES - IDX_DIM, 1) * INDEX_SCALE

    cos_m, sin_m = cm[...], sm[...]
    q = _dot(_rms(qd, qg[...]), wq_ref[...])
    scale = (QK_NOPE + QK_ROPE) ** -0.5
    for h in range(B_HEADS):
        lo = slice(h * QK_PAD, h * QK_PAD + LANES)
        hi = slice(h * QK_PAD + LANES, (h + 1) * QK_PAD)
        qb_o[:, lo] = (q[:, lo] * scale).astype(qb_o.dtype)
        qb_o[:, hi] = (_rope128(q[:, hi], cos_m, sin_m, QK_ROPE // 2, LANES) * scale).astype(qb_o.dtype)

    kv = _dot(_rms(kvd, kvg[...]), wkv_ref[...])
    krr = _rope128(kr, cos_m, sin_m, QK_ROPE // 2, LANES).astype(kb_o.dtype)
    for h in range(B_HEADS):
        kb_o[:, h * QK_PAD:h * QK_PAD + LANES] = kv[:, h * QK_NOPE:(h + 1) * QK_NOPE].astype(kb_o.dtype)
        kb_o[:, h * QK_PAD + LANES:(h + 1) * QK_PAD] = krr
    vb_o[...] = kv[:, B_HEADS * QK_NOPE:].astype(vb_o.dtype)


def _mla_prep(small, ci, si, cm, sm, qg, kvg, ikg, ikb, wq, wkv):
    n = small.shape[0]
    tm = _tile(n, 512)
    row = lambda i: (i, 0)
    const = lambda a: pl.BlockSpec(a.shape, lambda i: (0, 0))
    tab = pl.BlockSpec((tm, LANES), row)
    wide = B_HEADS * QK_PAD
    return pl.pallas_call(
        _mla_prep_kernel,
        out_shape=[jax.ShapeDtypeStruct((n, wide), _MXU_DTYPE), jax.ShapeDtypeStruct((n, wide), _MXU_DTYPE),
                   jax.ShapeDtypeStruct((n, B_WIDTH), _MXU_DTYPE),
                   jax.ShapeDtypeStruct((n, LANES), _MXU_DTYPE), jax.ShapeDtypeStruct((n, LANES), _MXU_DTYPE),
                   jax.ShapeDtypeStruct((n, LANES), jnp.float32)],
        grid=(n // tm,),
        in_specs=[pl.BlockSpec((tm, GROUP_W), row), tab, tab, tab, tab,
                  const(qg), const(kvg), const(ikg), const(ikb), const(wq), const(wkv)],
        out_specs=[pl.BlockSpec((tm, wide), row), pl.BlockSpec((tm, wide), row), pl.BlockSpec((tm, B_WIDTH), row),
                   tab, tab, tab],
        compiler_params=_cparams(("arbitrary",)),
        name="mla_prep",
    )(small, ci, si, cm, sm, qg, kvg, ikg, ikb, wq, wkv)


def _indexer_kernel(iq_ref, iklo_ref, ikhi_ref, iw_ref, mask_o, key_scr, *, tq, tk, topk):
    i = pl.program_id(1)
    n_kb_all = mask_o.shape[1]
    n_kb = ((i + 1) * tq + tk - 1) // tk
    q_chunk = (i * tq + lax.broadcasted_iota(jnp.int32, (tq, tk), 0)) // CHUNK
    col = lax.broadcasted_iota(jnp.int32, (tq, tk), 1)
    iw = iw_ref[0]

    def score_block(kb, carry):
        klo = iklo_ref[0, pl.ds(kb * tk, tk), :]
        khi = ikhi_ref[0, pl.ds(kb * tk, tk), :]
        sc = jnp.zeros((tq, tk), jnp.float32)
        for p in range(IDX_HEADS // 2):
            qp = iq_ref[0, :, p * LANES:(p + 1) * LANES]
            for half, kk in ((0, klo), (1, khi)):
                hd = 2 * p + half
                sc = sc + jnp.maximum(_dot_nt(qp, kk), 0.0) * iw[:, hd:hd + 1]
        bits = pltpu.bitcast(sc, jnp.int32)
        key = jnp.where(bits < 0, bits ^ 0x7FFFFFFF, bits)
        visible = (kb * tk + col) // CHUNK <= q_chunk
        key_scr[kb] = jnp.where(visible, key, INT_MIN)
        return carry

    lax.fori_loop(0, n_kb, score_block, 0)

    def bit_pass(it, t_u):
        cand_u = t_u | lax.shift_left(jnp.int32(1), jnp.asarray(31 - it, jnp.int32))
        cand_s = cand_u ^ INT_MIN

        def count_block(kb, acc):
            blk = key_scr[kb]
            for c in range(tk // LANES):
                acc = acc + jnp.where(blk[:, c * LANES:(c + 1) * LANES] >= cand_s, 1.0, 0.0)
            return acc

        acc = lax.fori_loop(0, n_kb, count_block, jnp.zeros((tq, LANES), jnp.float32))
        cnt = jnp.sum(acc, axis=1, keepdims=True)
        return jnp.where(cnt >= topk, cand_u, t_u)

    t_u = lax.fori_loop(0, 32, bit_pass, jnp.zeros((tq, 1), jnp.int32))
    t_s = jnp.maximum(t_u ^ INT_MIN, INT_MIN + 1)

    for kb in range(n_kb_all):
        @pl.when(kb < n_kb)
        def _():
            mask_o[0, kb] = jnp.where(key_scr[kb] >= t_s, 0.0, NEG_BIG).astype(mask_o.dtype)

        @pl.when(kb >= n_kb)
        def _():
            mask_o[0, kb] = jnp.full((tq, tk), NEG_BIG, mask_o.dtype)


def _indexer(iq3, iklo3, ikhi3, iw3, topk):
    b, s, _ = iq3.shape
    tq = _tile(s, 128)
    tk = _tile(s, 512)
    return pl.pallas_call(
        functools.partial(_indexer_kernel, tq=tq, tk=tk, topk=topk),
        out_shape=jax.ShapeDtypeStruct((b, s // tk, s, tk), jnp.bfloat16),
        grid=(b, s // tq),
        in_specs=[pl.BlockSpec((1, tq, GROUP_W), lambda bi, i: (bi, i, 0)),
                  pl.BlockSpec((1, s, LANES), lambda bi, i: (bi, 0, 0)),
                  pl.BlockSpec((1, s, LANES), lambda bi, i: (bi, 0, 0)),
                  pl.BlockSpec((1, tq, LANES), lambda bi, i: (bi, i, 0))],
        out_specs=pl.BlockSpec((1, s // tk, tq, tk), lambda bi, i: (bi, 0, i, 0)),
        scratch_shapes=[pltpu.VMEM((s // tk, tq, tk), jnp.int32)],
        compiler_params=_cparams(("arbitrary", "arbitrary")),
        name="indexer",
    )(iq3, iklo3, ikhi3, iw3)


def _softmax_heads(q_ref, k_ref, v_ref, heads, dqk, dv, n_plain, n_kb, tk, bias_fn, s_scr):
    tq = q_ref.shape[1]
    lane_tiles = [slice(c * LANES, (c + 1) * LANES) for c in range(tk // LANES)]

    def scores(masked):
        def body(kb, mxs):
            rows = pl.ds(pl.multiple_of(kb * tk, tk), tk)
            out = []
            for g, h in enumerate(heads):
                s = _dot_nt(q_ref[0, :, h * dqk:(h + 1) * dqk], k_ref[0, rows, h * dqk:(h + 1) * dqk])
                if masked:
                    s = bias_fn(kb, s)
                s_scr[g, kb] = s
                mx = mxs[g]
                for sl in lane_tiles:
                    mx = jnp.maximum(mx, s[:, sl])
                out.append(mx)
            return tuple(out)
        return body

    mxs = tuple(jnp.full((tq, LANES), NEG_BIG, jnp.float32) for _ in heads)
    mxs = lax.fori_loop(0, n_plain, scores(False), mxs)
    mxs = lax.fori_loop(n_plain, n_kb, scores(True), mxs)
    ms = [jnp.max(mx, axis=1, keepdims=True) for mx in mxs]

    def weigh(kb, carry):
        rows = pl.ds(pl.multiple_of(kb * tk, tk), tk)
        out = []
        for g, h in enumerate(heads):
            l, acc = carry[g]
            p = jnp.exp(s_scr[g, kb] - ms[g])
            for sl in lane_tiles:
                l = l + p[:, sl]
            out.append((l, acc + _dot(p, v_ref[0, rows, h * dv:(h + 1) * dv])))
        return tuple(out)

    zero = (jnp.zeros((tq, LANES), jnp.float32), jnp.zeros((tq, dv), jnp.float32))
    res = lax.fori_loop(0, n_kb, weigh, tuple(zero for _ in heads))
    return [acc / jnp.sum(l, axis=1, keepdims=True) for l, acc in res]


def _attn_a_kernel(q_ref, k_ref, v_ref, bias_ref, g_ref, o_ref, o_scr, s_scr, *, tq, tk):
    i = pl.program_id(1)
    n_kb = ((i + 1) * tq + tk - 1) // tk

    def bias_fn(kb, s):
        return s + bias_ref[0, kb].astype(jnp.float32)

    for h0 in range(0, A_HEADS, HEAD_GROUP):
        heads = list(range(h0, h0 + HEAD_GROUP))
        outs = _softmax_heads(q_ref, k_ref, v_ref, heads, A_HEAD_DIM, A_HEAD_DIM, 0, n_kb, tk, bias_fn, s_scr)
        for h, o in zip(heads, outs):
            o_scr[:, h * A_HEAD_DIM:(h + 1) * A_HEAD_DIM] = o
    o_ref[0] = _rms(o_scr[...], g_ref[...]).astype(o_ref.dtype)


def _attn_a(q3, k3, v3, mask4, g):
    b, s, w = q3.shape
    tk = mask4.shape[3]
    tq = _tile(s, 256)
    full = lambda bi, i: (bi, 0, 0)
    return pl.pallas_call(
        functools.partial(_attn_a_kernel, tq=tq, tk=tk),
        out_shape=jax.ShapeDtypeStruct((b, s, w), _MXU_DTYPE),
        grid=(b, s // tq),
        in_specs=[pl.BlockSpec((1, tq, w), lambda bi, i: (bi, i, 0)),
                  pl.BlockSpec((1, s, w), full, pipeline_mode=pl.Buffered(1)),
                  pl.BlockSpec((1, s, w), full, pipeline_mode=pl.Buffered(1)),
                  pl.BlockSpec((1, s // tk, tq, tk), lambda bi, i: (bi, 0, i, 0)),
                  pl.BlockSpec((1, w), lambda bi, i: (0, 0))],
        out_specs=pl.BlockSpec((1, tq, w), lambda bi, i: (bi, i, 0)),
        scratch_shapes=[pltpu.VMEM((tq, w), jnp.float32), pltpu.VMEM((HEAD_GROUP, s // tk, tq, tk), jnp.float32)],
        compiler_params=_cparams(("arbitrary", "arbitrary")),
        name="attn_a",
    )(q3, k3, v3, mask4, g)


def _attn_b_kernel(q_ref, k_ref, v_ref, g_ref, o_ref, o_scr, s_scr, *, tq, tk):
    i = pl.program_id(1)
    n_kb = ((i + 1) * tq + tk - 1) // tk
    n_plain = (i * tq) // tk
    q_chunk = (i * tq + lax.broadcasted_iota(jnp.int32, (tq, tk), 0)) // CHUNK
    col = lax.broadcasted_iota(jnp.int32, (tq, tk), 1)

    def bias_fn(kb, s):
        return jnp.where((kb * tk + col) // CHUNK <= q_chunk, s, NEG_BIG)

    for h0 in range(0, B_HEADS, HEAD_GROUP):
        heads = list(range(h0, h0 + HEAD_GROUP))
        outs = _softmax_heads(q_ref, k_ref, v_ref, heads, QK_PAD, V_HEAD, n_plain, n_kb, tk, bias_fn, s_scr)
        for h, o in zip(heads, outs):
            o_scr[:, h * V_HEAD:(h + 1) * V_HEAD] = o
    o_ref[0] = _rms(o_scr[...], g_ref[...]).astype(o_ref.dtype)


def _attn_b(q3, k3, v3, g):
    b, s, wq = q3.shape
    wv = v3.shape[2]
    tq = _tile(s, 256)
    tk = _tile(s, 512)
    full = lambda bi, i: (bi, 0, 0)
    return pl.pallas_call(
        functools.partial(_attn_b_kernel, tq=tq, tk=tk),
        out_shape=jax.ShapeDtypeStruct((b, s, wv), _MXU_DTYPE),
        grid=(b, s // tq),
        in_specs=[pl.BlockSpec((1, tq, wq), lambda bi, i: (bi, i, 0)),
                  pl.BlockSpec((1, s, wq), full, pipeline_mode=pl.Buffered(1)),
                  pl.BlockSpec((1, s, wv), full, pipeline_mode=pl.Buffered(1)),
                  pl.BlockSpec((1, wv), lambda bi, i: (0, 0))],
        out_specs=pl.BlockSpec((1, tq, wv), lambda bi, i: (bi, i, 0)),
        scratch_shapes=[pltpu.VMEM((tq, wv), jnp.float32), pltpu.VMEM((HEAD_GROUP, s // tk, tq, tk), jnp.float32)],
        compiler_params=_cparams(("arbitrary", "arbitrary")),
        name="attn_b",
    )(q3, k3, v3, g)


def _outproj_kernel(ma_ref, mb_ref, w_ref, x_ref, g1_ref, sc2_ref, sh2_ref, lg_ref, lb_ref, wrh_ref, wrl_ref, br_ref,
                    x1_o, h2_o, lgt_o, *, alpha, n_experts, n_split):
    rows_per = x_ref.shape[0] // n_split
    lane = lax.broadcasted_iota(jnp.int32, (rows_per, LANES), 1)
    for part in range(n_split):
        r = slice(part * rows_per, (part + 1) * rows_per)
        mix = (jnp.dot(ma_ref[r, :], w_ref[:A_WIDTH, :], preferred_element_type=jnp.float32)
               + jnp.dot(mb_ref[r, :], w_ref[A_WIDTH:, :], preferred_element_type=jnp.float32))
        x1 = _ln_plain(alpha * x_ref[r, :] + g1_ref[0] * mix) * lg_ref[...] + lb_ref[...]
        x1_o[r, :] = x1
        h2 = _ln_plain(x1) * (1.0 + sc2_ref[0]) + sh2_ref[0]
        h2_o[r, :] = h2
        h_hi = h2.astype(_MXU_DTYPE)
        h_lo = (h2 - h_hi.astype(jnp.float32)).astype(_MXU_DTYPE)
        logits = (jnp.dot(h_hi, wrh_ref[...], preferred_element_type=jnp.float32)
                  + jnp.dot(h_lo, wrh_ref[...], preferred_element_type=jnp.float32)
                  + jnp.dot(h_hi, wrl_ref[...], preferred_element_type=jnp.float32)) + br_ref[...]
        lgt_o[r, :] = jnp.where(lane < n_experts, logits, -jnp.inf)


def _outproj(ma, mb, w_out, x2, ada3, lg, lb, wr, br, seq, alpha, n_experts):
    n, d = x2.shape
    tm = _tile(seq, 512)
    n_split = 2 if tm % 32 == 0 else 1
    wr_hi = wr.astype(_MXU_DTYPE)
    wr_lo = (wr - wr_hi.astype(jnp.float32)).astype(_MXU_DTYPE)
    row = lambda i: (i, 0)
    const = lambda a: pl.BlockSpec(a.shape, lambda i: (0,) * a.ndim)
    ada_spec = lambda col: pl.BlockSpec((1, 1, d), lambda i, col=col: (i * tm // seq, 0, col))
    return pl.pallas_call(
        functools.partial(_outproj_kernel, alpha=alpha, n_experts=n_experts, n_split=n_split),
        out_shape=[jax.ShapeDtypeStruct((n, d), jnp.float32),
                   jax.ShapeDtypeStruct((n, d), jnp.float32),
                   jax.ShapeDtypeStruct((n, LANES), jnp.float32)],
        grid=(n // tm,),
        in_specs=[pl.BlockSpec((tm, A_WIDTH), row), pl.BlockSpec((tm, B_WIDTH), row),
                  pl.BlockSpec(w_out.shape, lambda i: (0, 0), pipeline_mode=pl.Buffered(1)),
                  pl.BlockSpec((tm, d), row), ada_spec(2), ada_spec(4), ada_spec(3),
                  const(lg), const(lb), const(wr_hi), const(wr_lo), const(br)],
        out_specs=[pl.BlockSpec((tm, d), row), pl.BlockSpec((tm, d), row), pl.BlockSpec((tm, LANES), row)],
        compiler_params=_cparams(("arbitrary",)),
        name="outproj",
    )(ma, mb, w_out, x2, ada3, ada3, ada3, lg, lb, wr_hi, wr_lo, br)


def _route_kernel(lgt_ref, top_o, gate_o, rank_o, cnt_o, carry):
    tb = lgt_ref.shape[0]

    @pl.when(pl.program_id(0) == 0)
    def _():
        carry[...] = jnp.zeros_like(carry)

    lane = lax.broadcasted_iota(jnp.int32, (tb, LANES), 1)
    lane_f = lane.astype(jnp.float32)
    work = lgt_ref[...]
    vals, idxs, hots = [], [], []
    for _ in range(TOP_K_EXPERTS):
        m = jnp.max(work, axis=1, keepdims=True)
        idx = jnp.min(jnp.where(work == m, lane_f, float(LANES)), axis=1, keepdims=True)
        hot = lane_f == idx
        vals.append(m)
        idxs.append(idx)
        hots.append(hot)
        work = jnp.where(hot, -jnp.inf, work)

    exps = [jnp.exp(v - vals[0]) for v in vals]
    denom = exps[0] + exps[1] + exps[2] + exps[3]

    member = jnp.zeros((tb, LANES), jnp.float32)
    for hot in hots:
        member = member + jnp.where(hot, 1.0, 0.0)
    r = lax.broadcasted_iota(jnp.int32, (tb, tb), 0)
    c = lax.broadcasted_iota(jnp.int32, (tb, tb), 1)
    before = jnp.where(c < r, 1.0, 0.0).astype(jnp.bfloat16)
    prefix = jnp.dot(before, member.astype(jnp.bfloat16), preferred_element_type=jnp.float32) + carry[...]

    top = jnp.zeros((tb, LANES), jnp.int32)
    gate = jnp.zeros((tb, LANES), jnp.float32)
    rank = jnp.zeros((tb, LANES), jnp.int32)
    for k in range(TOP_K_EXPERTS):
        rk = jnp.sum(jnp.where(hots[k], prefix, 0.0), axis=1, keepdims=True)
        top = jnp.where(lane == k, idxs[k].astype(jnp.int32), top)
        gate = jnp.where(lane == k, exps[k] / denom, gate)
        rank = jnp.where(lane == k, rk.astype(jnp.int32), rank)
    top_o[...] = top
    gate_o[...] = gate
    rank_o[...] = rank
    carry[...] = carry[...] + jnp.sum(member, axis=0, keepdims=True)
    cnt_o[...] = jnp.broadcast_to(carry[...], cnt_o.shape).astype(jnp.int32)


def _route(logits):
    n = logits.shape[0]
    tb = _tile(n, 512)
    row = pl.BlockSpec((tb, LANES), lambda i: (i, 0))
    return pl.pallas_call(
        _route_kernel,
        out_shape=[jax.ShapeDtypeStruct((n, LANES), jnp.int32), jax.ShapeDtypeStruct((n, LANES), jnp.float32),
                   jax.ShapeDtypeStruct((n, LANES), jnp.int32), jax.ShapeDtypeStruct((8, LANES), jnp.int32)],
        grid=(n // tb,),
        in_specs=[row],
        out_specs=[row, row, row, pl.BlockSpec((8, LANES), lambda i: (0, 0))],
        scratch_shapes=[pltpu.VMEM((1, LANES), jnp.float32)],
        compiler_params=_cparams(("arbitrary",)),
        name="route",
    )(logits)


def _row_copy(src, s_row, dst, d_row, sem):
    return pltpu.make_async_copy(src.at[pl.ds(s_row, 1), :], dst.at[pl.ds(d_row, 1), :], sem)


def _dispatch_kernel(dest_ref, h_ref, xs_in, xs_hbm, sem, *, td):
    del xs_in
    base = pl.program_id(0) * td

    def issue(t, c):
        for k in range(TOP_K_EXPERTS):
            _row_copy(h_ref, t, xs_hbm, dest_ref[(base + t) * TOP_K_EXPERTS + k], sem).start()
        return c

    lax.fori_loop(0, td, issue, 0, unroll=ROW_DMA_UNROLL)
    landed = xs_hbm.at[pl.ds(0, td * TOP_K_EXPERTS), :]
    pltpu.make_async_copy(landed, landed, sem).wait()


def _dispatch(dest_flat, h2, rows):
    n, d = h2.shape
    td = _tile(n, 256)
    xs0 = jnp.zeros((rows, d), h2.dtype)
    return pl.pallas_call(
        functools.partial(_dispatch_kernel, td=td),
        out_shape=jax.ShapeDtypeStruct(xs0.shape, xs0.dtype),
        grid_spec=pltpu.PrefetchScalarGridSpec(
            num_scalar_prefetch=1, grid=(n // td,),
            in_specs=[pl.BlockSpec((td, d), lambda i, dr: (i, 0)), pl.BlockSpec(memory_space=pl.ANY)],
            out_specs=pl.BlockSpec(memory_space=pl.ANY),
            scratch_shapes=[pltpu.SemaphoreType.DMA(())]),
        input_output_aliases={2: 0},
        compiler_params=pltpu.CompilerParams(dimension_semantics=("arbitrary",), has_side_effects=True,
                                             disable_bounds_checks=True),
        name="dispatch",
    )(dest_flat, h2, xs0)


def _swiglu_compact(gu):
    tf2 = gu.shape[1]
    nxt = pltpu.roll(gu, tf2 - 1, 1)
    g = jnp.minimum(gu, SWIGLU_LIMIT)
    u = jnp.clip(nxt, -SWIGLU_LIMIT, SWIGLU_LIMIT)
    act = ((u + 1.0) * (g * jax.nn.sigmoid(SWIGLU_ALPHA * g))).astype(_MXU_DTYPE)
    r = lax.broadcasted_iota(jnp.int32, (2 * LANES, LANES), 0)
    c = lax.broadcasted_iota(jnp.int32, (2 * LANES, LANES), 1)
    pick = jnp.where(r == 2 * c, 1.0, 0.0).astype(_MXU_DTYPE)
    parts = [jnp.dot(act[:, t * 2 * LANES:(t + 1) * 2 * LANES], pick, preferred_element_type=jnp.float32)
             for t in range(tf2 // (2 * LANES))]
    return jnp.concatenate(parts, axis=1).astype(_MXU_DTYPE)


def _moe_gemm_kernel(sbe_ref, row0_ref, nsub_ref, xs_hbm, wgu_ref, bgu_ref, wd_ref, bd_ref, y_hbm,
                     xstage, xb, act, wb, wdb, ystage, gu_scr, sem_x, sem_y, *, sub, n_f):
    del sbe_ref
    sb = pl.program_id(0)
    t = pl.program_id(1)
    nsub = nsub_ref[sb]
    row0 = row0_ref[sb]
    tf = wd_ref.shape[1]

    def x_copy(c, slot):
        rows = pl.ds(pl.multiple_of(row0 + c * sub, sub), sub)
        return pltpu.make_async_copy(xs_hbm.at[rows, :], xstage.at[slot], sem_x.at[slot])

    def y_copy(s, slot):
        rows = pl.ds(pl.multiple_of(row0 + s * sub, sub), sub)
        return pltpu.make_async_copy(ystage.at[slot], y_hbm.at[rows, :], sem_y.at[slot])

    @pl.when(jnp.logical_and(sb == 0, t == 0))
    def _():
        xb[...] = jnp.zeros_like(xb)

    @pl.when(jnp.logical_and(nsub > 0, t < n_f))
    def _():
        wb[...] = wgu_ref[0].astype(wb.dtype)
        wdb[pl.ds(pl.multiple_of(t * tf, tf), tf), :] = wd_ref[0].astype(wdb.dtype)
        bias = bgu_ref[0]

        def gate_up(x):
            return jnp.dot(x, wb[...], preferred_element_type=jnp.float32) + bias

        @pl.when(t == 0)
        def _():
            x_copy(0, 0).start()

            def first(c, carry):
                slot = c % 2

                @pl.when(c + 1 < nsub)
                def _():
                    x_copy(c + 1, 1 - slot).start()

                x_copy(c, slot).wait()
                rows = pl.ds(pl.multiple_of(c * sub, sub), sub)
                xv = xstage[slot].astype(xb.dtype)
                xb[rows, :] = xv
                act[0, rows, :] = _swiglu_compact(gate_up(xv))
                return carry

            lax.fori_loop(0, nsub, first, 0)

        @pl.when(t > 0)
        def _():
            n_trips = (nsub + 1) // 2
            trip_rows = lambda p: pl.ds(pl.multiple_of(p * 2 * sub, 2 * sub), 2 * sub)
            gu_scr[...] = gate_up(xb[trip_rows(0), :])

            def up(p, carry):
                prev = gu_scr[...]
                gu_scr[...] = gate_up(xb[trip_rows(p), :])
                act[t, trip_rows(p - 1), :] = _swiglu_compact(prev)
                return carry

            lax.fori_loop(1, n_trips, up, 0)
            act[t, trip_rows(n_trips - 1), :] = _swiglu_compact(gu_scr[...])

    @pl.when(jnp.logical_and(nsub > 0, t == n_f))
    def _():
        bias = bd_ref[0]

        def down(s, carry):
            slot = s % 2
            rows = pl.ds(pl.multiple_of(s * sub, sub), sub)
            a = jnp.concatenate([act[f, rows, :] for f in range(n_f)], axis=1)
            yt = jnp.dot(a, wdb[...], preferred_element_type=jnp.float32) + bias

            @pl.when(s >= 2)
            def _():
                y_copy(s - 2, slot).wait()

            ystage[slot] = yt
            y_copy(s, slot).start()
            return carry

        lax.fori_loop(0, nsub, down, 0)

        @pl.when(nsub >= 2)
        def _():
            y_copy(nsub - 2, nsub % 2).wait()

        y_copy(nsub - 1, (nsub - 1) % 2).wait()


def _moe_gemm(sb_e, sb_row0, sb_nsub, xs, wgu, bgu3, wd, bd3, sub, max_sub):
    e, d, ff2 = wgu.shape
    ff = ff2 // 2
    tf = _tile(ff, 256)
    n_f = ff // tf
    n_sb = sb_e.shape[0]
    rm = (max_sub + 1) // 2 * 2 * sub
    f_idx = lambda sb, t, ns: jnp.where(ns[sb] > 0, jnp.minimum(t, n_f - 1), n_f - 1)
    return pl.pallas_call(
        functools.partial(_moe_gemm_kernel, sub=sub, n_f=n_f),
        out_shape=jax.ShapeDtypeStruct(xs.shape, jnp.float32),
        grid_spec=pltpu.PrefetchScalarGridSpec(
            num_scalar_prefetch=3, grid=(n_sb, n_f + 1),
            in_specs=[pl.BlockSpec(memory_space=pl.ANY),
                      pl.BlockSpec((1, d, 2 * tf), lambda sb, t, se, r0, ns: (se[sb], 0, f_idx(sb, t, ns))),
                      pl.BlockSpec((1, 1, 2 * tf), lambda sb, t, se, r0, ns: (se[sb], 0, f_idx(sb, t, ns))),
                      pl.BlockSpec((1, tf, d), lambda sb, t, se, r0, ns: (se[sb], f_idx(sb, t, ns), 0)),
                      pl.BlockSpec((1, 1, d), lambda sb, t, se, r0, ns: (se[sb], 0, 0))],
            out_specs=pl.BlockSpec(memory_space=pl.ANY),
            scratch_shapes=[pltpu.VMEM((2, sub, d), jnp.float32),
                            pltpu.VMEM((rm, d), _MXU_DTYPE),
                            pltpu.VMEM((n_f, rm, tf), _MXU_DTYPE),
                            pltpu.VMEM((d, 2 * tf), _MXU_DTYPE),
                            pltpu.VMEM((ff, d), _MXU_DTYPE),
                            pltpu.VMEM((2, sub, d), jnp.float32),
                            pltpu.VMEM((2 * sub, 2 * tf), jnp.float32),
                            pltpu.SemaphoreType.DMA((2,)), pltpu.SemaphoreType.DMA((2,))]),
        input_output_aliases={3: 0},
        compiler_params=_cparams(("arbitrary", "arbitrary")),
        name="moe_gemm",
    )(sb_e, sb_row0, sb_nsub, xs, wgu, bgu3, wd, bd3)


def _combine_kernel(dest_ref, y_hbm, gate_ref, x1_ref, g2_ref, lg_ref, lb_ref, o_ref, buf, sem, *, tc, alpha):
    base = pl.program_id(0) * tc

    def issue(t, c):
        for k in range(TOP_K_EXPERTS):
            _row_copy(y_hbm, dest_ref[(base + t) * TOP_K_EXPERTS + k], buf, k * tc + t, sem).start()
        return c

    lax.fori_loop(0, tc, issue, 0, unroll=ROW_DMA_UNROLL)
    pltpu.make_async_copy(y_hbm.at[pl.ds(0, tc * TOP_K_EXPERTS), :], buf, sem).wait()

    gate = gate_ref[...]
    ffn = buf[0:tc, :] * gate[:, 0:1]
    for k in range(1, TOP_K_EXPERTS):
        ffn = ffn + buf[k * tc:(k + 1) * tc, :] * gate[:, k:k + 1]
    o_ref[...] = _ln_plain(alpha * x1_ref[...] + g2_ref[0] * ffn) * lg_ref[...] + lb_ref[...]


def _combine(dest_flat, y, gates, x1, ada3, lg, lb, seq, alpha):
    n, d = x1.shape
    tc = _tile(seq, 256)
    row = lambda i, dr: (i, 0)
    return pl.pallas_call(
        functools.partial(_combine_kernel, tc=tc, alpha=alpha),
        out_shape=jax.ShapeDtypeStruct((n, d), jnp.float32),
        grid_spec=pltpu.PrefetchScalarGridSpec(
            num_scalar_prefetch=1, grid=(n // tc,),
            in_specs=[pl.BlockSpec(memory_space=pl.ANY),
                      pl.BlockSpec((tc, LANES), row), pl.BlockSpec((tc, d), row),
                      pl.BlockSpec((1, 1, d), lambda i, dr: (i * tc // seq, 0, 5)),
                      pl.BlockSpec((1, d), lambda i, dr: (0, 0)), pl.BlockSpec((1, d), lambda i, dr: (0, 0))],
            out_specs=pl.BlockSpec((tc, d), row),
            scratch_shapes=[pltpu.VMEM((TOP_K_EXPERTS * tc, d), jnp.float32),
                            pltpu.SemaphoreType.DMA(())]),
        compiler_params=pltpu.CompilerParams(dimension_semantics=("arbitrary",), vmem_limit_bytes=VMEM_LIMIT,
                                             disable_bounds_checks=True),
        name="combine",
    )(dest_flat, y, gates, x1, ada3, lg, lb)


def _pad_cols(a, width):
    return jnp.pad(a, ((0, 0), (0, width - a.shape[1])))


def _perm_w_in(w_in):
    o = np.cumsum([0, A_WIDTH, A_WIDTH, A_WIDTH, IDX_HEADS * IDX_DIM, IDX_DIM, IDX_HEADS, Q_LORA, KV_LORA, QK_ROPE])
    seg = lambda i: w_in[:, o[i]:o[i + 1]]
    ikw = _pad_cols(jnp.concatenate([seg(4), seg(5)], axis=1), LANES)
    kr = _pad_cols(seg(8), LANES)
    return jnp.concatenate([seg(0), seg(1), seg(2), seg(3), seg(6), seg(7), ikw, kr], axis=1)


def _perm_w_q_up(w):
    w = w.reshape(Q_LORA, B_HEADS, QK_NOPE + QK_ROPE)
    w = jnp.pad(w, ((0, 0), (0, 0), (0, QK_PAD - QK_NOPE - QK_ROPE)))
    return w.reshape(Q_LORA, B_HEADS * QK_PAD)


def _perm_w_kv_up(w):
    w = w.reshape(KV_LORA, B_HEADS, QK_NOPE + V_HEAD)
    return jnp.concatenate([w[:, :, :QK_NOPE].reshape(KV_LORA, -1), w[:, :, QK_NOPE:].reshape(KV_LORA, -1)], axis=1)


MOE_MAX_SUB = 9


def _moe_sub_rows(n_tokens):
    return 256 if n_tokens >= 4096 else 64


def _super_blocks(cnt, sub, n_assign):
    n_experts = cnt.shape[0]
    q = (cnt + sub - 1) // sub
    pad_ends = jnp.cumsum(q * sub)
    pad_starts = pad_ends - q * sub
    nsb_e = (q + MOE_MAX_SUB - 1) // MOE_MAX_SUB
    sb_end = jnp.cumsum(nsb_e)
    sb_start = sb_end - nsb_e
    n_sb = (n_assign // sub + n_experts + MOE_MAX_SUB - 1) // MOE_MAX_SUB + n_experts
    idx = jnp.arange(n_sb, dtype=jnp.int32)
    valid = idx < sb_end[-1]
    e_of = jnp.minimum(jnp.sum((sb_end[None, :] <= idx[:, None]).astype(jnp.int32), axis=1), n_experts - 1)
    j = idx - sb_start[e_of]
    parts = jnp.maximum(nsb_e[e_of], 1)
    base, rem = q[e_of] // parts, q[e_of] % parts
    nsub = jnp.where(valid, base + (j < rem).astype(jnp.int32), 0)
    row0 = jnp.where(valid, pad_starts[e_of] + (j * base + jnp.minimum(j, rem)) * sub, 0)
    e_last = jnp.max(jnp.where(valid, e_of, 0))
    sb_e = jnp.where(valid, e_of, e_last)
    i32 = lambda a: a.astype(jnp.int32)
    return pad_starts, i32(sb_e), i32(row0), i32(nsub)


def kernel(x, c, positions, w_ada, b_ada, w_in, idx_k_norm_g, idx_k_norm_b, q_norm_g, w_q_up, kv_norm_g, w_kv_up,
           out_norm_a_g, out_norm_b_g, w_out, ln_mix_g, ln_mix_b, w_router, b_router, w_gate_up, b_gate_up,
           w_down, b_down, ln_ffn_g, ln_ffn_b):
    bsz, seq, d = x.shape
    depth = w_ada.shape[0]
    n_experts = w_router.shape[2]
    n = bsz * seq
    alpha = (2 * depth) ** 0.25
    topk = min(TOPK_MAX, seq // 4)
    sub = _moe_sub_rows(n)
    n_assign = n * TOP_K_EXPERTS
    rows = (n_assign // sub + n_experts) * sub

    ca, sa, ci, si, cm, sm = _rope_tables(positions.reshape(n, 1))
    c8 = jnp.pad(c, ((0, 8 - bsz), (0, 0)))
    x2 = x.reshape(n, d)
    row = lambda a: a.reshape(1, -1)

    for l in range(depth):
        ada3 = _ada(c8, w_ada[l], row(b_ada[l]))[:bsz].reshape(bsz, 1, 6 * d)

        aq, ak, av, iq, small = _proj(x2, ada3, _perm_w_in(w_in[l]).astype(_MXU_DTYPE), ca, sa, ci, si, seq)
        qb, kb, vb, iklo, ikhi, iw = _mla_prep(
            small, ci, si, cm, sm, row(q_norm_g[l]), row(kv_norm_g[l]),
            _pad_cols(row(idx_k_norm_g[l]), LANES), _pad_cols(row(idx_k_norm_b[l]), LANES),
            _perm_w_q_up(w_q_up[l]).astype(_MXU_DTYPE), _perm_w_kv_up(w_kv_up[l]).astype(_MXU_DTYPE))
        b3 = lambda a: a.reshape(bsz, seq, a.shape[-1])
        mask = _indexer(b3(iq), b3(iklo), b3(ikhi), b3(iw), topk)
        out_a = _attn_a(b3(aq), b3(ak), b3(av), mask, row(out_norm_a_g[l]))
        out_b = _attn_b(b3(qb), b3(kb), b3(vb), row(out_norm_b_g[l]))

        x1, h2, logits = _outproj(
            out_a.reshape(n, A_WIDTH), out_b.reshape(n, B_WIDTH), w_out[l].astype(_MXU_DTYPE), x2, ada3,
            row(ln_mix_g[l]), row(ln_mix_b[l]), _pad_cols(w_router[l], LANES), _pad_cols(row(b_router[l]), LANES),
            seq, alpha, n_experts)
        top, gates, rank, counts = _route(logits)

        pad_starts, sb_e, sb_row0, sb_nsub = _super_blocks(counts[0, :n_experts], sub, n_assign)
        dest = (pad_starts[top[:, :TOP_K_EXPERTS]] + rank[:, :TOP_K_EXPERTS]).reshape(-1).astype(jnp.int32)

        xs = _dispatch(dest, h2, rows)
        y = _moe_gemm(sb_e, sb_row0, sb_nsub, xs, w_gate_up[l], b_gate_up[l].reshape(n_experts, 1, -1), w_down[l],
                      b_down[l].reshape(n_experts, 1, -1), sub, MOE_MAX_SUB)
        x2 = _combine(dest, y, gates, x1, ada3, row(ln_ffn_g[l]), row(ln_ffn_b[l]), seq, alpha)

    return x2.reshape(bsz, seq, d)
```

```python
import functools
import math

import jax
import jax.numpy as jnp
import numpy as np
from jax import lax
from jax.experimental import pallas as pl
from jax.experimental.pallas import tpu as pltpu

CHUNK = 64
ROPE_THETA = 500000.0
EPS = 1e-5
A_HEADS = 8
A_HEAD_DIM = 128
A_ROT_DIM = A_HEAD_DIM // 4
IDX_HEADS = 16
IDX_DIM = 64
IDX_ROT_DIM = IDX_DIM // 4
TOPK_MAX = 256
INDEX_SCALE = (IDX_DIM ** -0.5) * (IDX_HEADS ** -0.5)
B_HEADS = 8
Q_LORA = 512
KV_LORA = 256
QK_NOPE = 128
QK_ROPE = 64
V_HEAD = 128
A_WIDTH = A_HEADS * A_HEAD_DIM
B_WIDTH = B_HEADS * V_HEAD
TOP_K_EXPERTS = 4
SWIGLU_LIMIT = 7.0
SWIGLU_ALPHA = 1.702

LANES = 128
QK_PAD = 256
GROUP_W = 1024
VMEM_LIMIT = 56 * 1024 * 1024
NEG_BIG = -1e30
KEY_BLOCK = 1024
HEAD_GROUP = 4
ROW_DMA_UNROLL = 8
INT_MIN = -2 ** 31

_MXU_DTYPE = jnp.bfloat16


def _cparams(sem, vmem=VMEM_LIMIT):
    return pltpu.CompilerParams(dimension_semantics=sem, vmem_limit_bytes=vmem)


def _tile(n, t):
    t = min(n, t)
    assert n % t == 0, (n, t)
    return t


def _dot(a, b):
    return jnp.dot(a.astype(_MXU_DTYPE), b.astype(_MXU_DTYPE), preferred_element_type=jnp.float32)


def _dot_nt(a, b):
    return lax.dot_general(a.astype(_MXU_DTYPE), b.astype(_MXU_DTYPE), (((1,), (1,)), ((), ())),
                           preferred_element_type=jnp.float32)


def _ln_plain(x):
    mu = jnp.mean(x, axis=-1, keepdims=True)
    d = x - mu
    var = jnp.mean(d * d, axis=-1, keepdims=True)
    return d * lax.rsqrt(var + EPS)


def _rms(x, g):
    return x * lax.rsqrt(jnp.mean(x * x, axis=-1, keepdims=True) + EPS) * g


def _rope128(x, cos, sin, half, group):
    lane = lax.broadcasted_iota(jnp.int32, x.shape, 1) % group
    partner = jnp.where(lane < half, pltpu.roll(x, LANES - half, 1), pltpu.roll(x, half, 1))
    return x * cos + partner * sin


def _ada_kernel(c_ref, w_ref, b_ref, o_ref):
    c = c_ref[...]
    o_ref[...] = _dot(c * jax.nn.sigmoid(c), w_ref[...]) + b_ref[...]


def _ada(c8, w, b):
    d, n = w.shape
    tn = math.gcd(n, 1024)
    return pl.pallas_call(
        _ada_kernel,
        out_shape=jax.ShapeDtypeStruct((8, n), jnp.float32),
        grid=(n // tn,),
        in_specs=[pl.BlockSpec((8, d), lambda j: (0, 0)),
                  pl.BlockSpec((d, tn), lambda j: (0, j)),
                  pl.BlockSpec((1, tn), lambda j: (0, j))],
        out_specs=pl.BlockSpec((8, tn), lambda j: (0, j)),
        compiler_params=_cparams(("arbitrary",)),
        name="ada",
    )(c8, w, b)


def _rope_rows(rot, group):
    half = rot // 2
    inv_freq = ROPE_THETA ** (-jnp.arange(half, dtype=jnp.float32) / half)
    lane = np.arange(LANES) % group
    freq = jnp.where(lane < rot, inv_freq[lane % half], 0.0)
    sign = np.where(lane < half, -1.0, np.where(lane < rot, 1.0, 0.0)).astype(np.float32)
    return freq.reshape(1, LANES).astype(jnp.float32), jnp.asarray(sign).reshape(1, LANES)


def _rope_tables_kernel(pos_ref, fa, sa, fi, si, fm, sm, ca_o, sa_o, ci_o, si_o, cm_o, sm_o):
    pos = pos_ref[...].astype(jnp.float32)
    for f, s, c_o, s_o in ((fa, sa, ca_o, sa_o), (fi, si, ci_o, si_o), (fm, sm, cm_o, sm_o)):
        ang = pos * f[...]
        c_o[...] = jnp.cos(ang)
        s_o[...] = jnp.sin(ang) * s[...]


def _rope_tables(pos_col):
    n = pos_col.shape[0]
    t = _tile(n, 1024)
    rows = (*_rope_rows(A_ROT_DIM, A_HEAD_DIM), *_rope_rows(IDX_ROT_DIM, IDX_DIM), *_rope_rows(QK_ROPE, LANES))
    row_spec = pl.BlockSpec((1, LANES), lambda i: (0, 0))
    tab_spec = pl.BlockSpec((t, LANES), lambda i: (i, 0))
    return pl.pallas_call(
        _rope_tables_kernel,
        out_shape=[jax.ShapeDtypeStruct((n, LANES), jnp.float32)] * 6,
        grid=(n // t,),
        in_specs=[pl.BlockSpec((t, 1), lambda i: (i, 0))] + [row_spec] * 6,
        out_specs=[tab_spec] * 6,
        compiler_params=_cparams(("arbitrary",)),
        name="rope_tables",
    )(pos_col, *rows)


def _proj_kernel(x_ref, sc_ref, sh_ref, w_ref, ca, sa, ci, si, aq_o, ak_o, av_o, iq_o, sm_o):
    h = (_ln_plain(x_ref[...]) * (1.0 + sc_ref[0]) + sh_ref[0]).astype(w_ref.dtype)

    def group(j):
        return jnp.dot(h, w_ref[:, j * GROUP_W:(j + 1) * GROUP_W], preferred_element_type=jnp.float32)

    def rope_heads(acc, o_ref, cos_ref, sin_ref, half, group_w, scale):
        cos, sin = cos_ref[...], sin_ref[...]
        for t in range(GROUP_W // LANES):
            sl = slice(t * LANES, (t + 1) * LANES)
            o_ref[:, sl] = (_rope128(acc[:, sl], cos, sin, half, group_w) * scale).astype(o_ref.dtype)

    rope_heads(group(0), aq_o, ca, sa, A_ROT_DIM // 2, A_HEAD_DIM, A_HEAD_DIM ** -0.5)
    rope_heads(group(1), ak_o, ca, sa, A_ROT_DIM // 2, A_HEAD_DIM, 1.0)
    av_o[...] = group(2).astype(av_o.dtype)
    rope_heads(group(3), iq_o, ci, si, IDX_ROT_DIM // 2, IDX_DIM, 1.0)
    sm_o[...] = group(4)


def _proj(x2, ada3, w_perm, ca, sa, ci, si, seq):
    n, d = x2.shape
    tm = _tile(seq, 512)
    row = lambda i: (i, 0)
    ada_spec = lambda col: pl.BlockSpec((1, 1, d), lambda i, col=col: (i * tm // seq, 0, col))
    big = lambda dt: jax.ShapeDtypeStruct((n, GROUP_W), dt)
    return pl.pallas_call(
        _proj_kernel,
        out_shape=[big(_MXU_DTYPE)] * 4 + [big(jnp.float32)],
        grid=(n // tm,),
        in_specs=[pl.BlockSpec((tm, d), row), ada_spec(1), ada_spec(0),
                  pl.BlockSpec(w_perm.shape, lambda i: (0, 0), pipeline_mode=pl.Buffered(1))]
                 + [pl.BlockSpec((tm, LANES), row)] * 4,
        out_specs=[pl.BlockSpec((tm, GROUP_W), row)] * 5,
        compiler_params=_cparams(("arbitrary",)),
        name="proj",
    )(x2, ada3, ada3, w_perm, ca, sa, ci, si)


def _mla_prep_kernel(sm_ref, ci, si, cm, sm, qg, kvg, ikg, ikb, wq_ref, wkv_ref,
                     qb_o, kb_o, vb_o, iklo_o, ikhi_o, iw_o):
    small = sm_ref[...]
    qd = small[:, :Q_LORA]
    kvd = small[:, Q_LORA:Q_LORA + KV_LORA]
    ikw = small[:, Q_LORA + KV_LORA:Q_LORA + KV_LORA + LANES]
    kr = small[:, Q_LORA + KV_LORA + LANES:]

    lane = lax.broadcasted_iota(jnp.int32, ikw.shape, 1)
    is_k = lane < IDX_DIM
    ik = jnp.where(is_k, ikw, 0.0)
    mu = jnp.sum(ik, axis=-1, keepdims=True) * (1.0 / IDX_DIM)
    dk = jnp.where(is_k, ikw - mu, 0.0)
    var = jnp.sum(dk * dk, axis=-1, keepdims=True) * (1.0 / IDX_DIM)
    ik = dk * lax.rsqrt(var + EPS) * ikg[...] + ikb[...]
    ik = _rope128(ik, ci[...], si[...], IDX_ROT_DIM // 2, IDX_DIM)
    ik = jnp.where(is_k, ik, 0.0)
    iklo_o[...] = ik.astype(iklo_o.dtype)
    ikhi_o[...] = pltpu.roll(ik, IDX_DIM, 1).astype(ikhi_o.dtype)
    iw_o[...] = pltpu.roll(ikw, LANES - IDX_DIM, 1) * INDEX_SCALE

    cos_m, sin_m = cm[...], sm[...]
    q = _dot(_rms(qd, qg[...]), wq_ref[...])
    scale = (QK_NOPE + QK_ROPE) ** -0.5
    for h in range(B_HEADS):
        lo = slice(h * QK_PAD, h * QK_PAD + LANES)
        hi = slice(h * QK_PAD + LANES, (h + 1) * QK_PAD)
        qb_o[:, lo] = (q[:, lo] * scale).astype(qb_o.dtype)
        qb_o[:, hi] = (_rope128(q[:, hi], cos_m, sin_m, QK_ROPE // 2, LANES) * scale).astype(qb_o.dtype)

    kv = _dot(_rms(kvd, kvg[...]), wkv_ref[...])
    krr = _rope128(kr, cos_m, sin_m, QK_ROPE // 2, LANES).astype(kb_o.dtype)
    for h in range(B_HEADS):
        kb_o[:, h * QK_PAD:h * QK_PAD + LANES] = kv[:, h * QK_NOPE:(h + 1) * QK_NOPE].astype(kb_o.dtype)
        kb_o[:, h * QK_PAD + LANES:(h + 1) * QK_PAD] = krr
    vb_o[...] = kv[:, B_HEADS * QK_NOPE:].astype(vb_o.dtype)


def _mla_prep(small, ci, si, cm, sm, qg, kvg, ikg, ikb, wq, wkv):
    n = small.shape[0]
    tm = _tile(n, 512)
    row = lambda i: (i, 0)
    const = lambda a: pl.BlockSpec(a.shape, lambda i: (0, 0))
    tab = pl.BlockSpec((tm, LANES), row)
    wide = B_HEADS * QK_PAD
    return pl.pallas_call(
        _mla_prep_kernel,
        out_shape=[jax.ShapeDtypeStruct((n, wide), _MXU_DTYPE), jax.ShapeDtypeStruct((n, wide), _MXU_DTYPE),
                   jax.ShapeDtypeStruct((n, B_WIDTH), _MXU_DTYPE),
                   jax.ShapeDtypeStruct((n, LANES), _MXU_DTYPE), jax.ShapeDtypeStruct((n, LANES), _MXU_DTYPE),
                   jax.ShapeDtypeStruct((n, LANES), jnp.float32)],
        grid=(n // tm,),
        in_specs=[pl.BlockSpec((tm, GROUP_W), row), tab, tab, tab, tab,
                  const(qg), const(kvg), const(ikg), const(ikb), const(wq), const(wkv)],
        out_specs=[pl.BlockSpec((tm, wide), row), pl.BlockSpec((tm, wide), row), pl.BlockSpec((tm, B_WIDTH), row),
                   tab, tab, tab],
        compiler_params=_cparams(("arbitrary",)),
        name="mla_prep",
    )(small, ci, si, cm, sm, qg, kvg, ikg, ikb, wq, wkv)


def _indexer_kernel(iq_ref, iklo_ref, ikhi_ref, iw_ref, mask_o, key_scr, *, tq, tk, topk):
    i = pl.program_id(1)
    n_kb_all = mask_o.shape[1]
    n_kb = ((i + 1) * tq + tk - 1) // tk
    q_chunk = (i * tq + lax.broadcasted_iota(jnp.int32, (tq, tk), 0)) // CHUNK
    col = lax.broadcasted_iota(jnp.int32, (tq, tk), 1)
    iw = iw_ref[0]

    def score_block(kb, carry):
        klo = iklo_ref[0, pl.ds(kb * tk, tk), :]
        khi = ikhi_ref[0, pl.ds(kb * tk, tk), :]
        sc = jnp.zeros((tq, tk), jnp.float32)
        for p in range(IDX_HEADS // 2):
            qp = iq_ref[0, :, p * LANES:(p + 1) * LANES]
            for half, kk in ((0, klo), (1, khi)):
                hd = 2 * p + half
                sc = sc + jnp.maximum(_dot_nt(qp, kk), 0.0) * iw[:, hd:hd + 1]
        bits = pltpu.bitcast(sc, jnp.int32)
        key = jnp.where(bits < 0, bits ^ 0x7FFFFFFF, bits)
        visible = (kb * tk + col) // CHUNK <= q_chunk
        key_scr[kb] = jnp.where(visible, key, INT_MIN)
        return carry

    lax.fori_loop(0, n_kb, score_block, 0)

    def bit_pass(it, t_u):
        cand_u = t_u | lax.shift_left(jnp.int32(1), jnp.asarray(31 - it, jnp.int32))
        cand_s = cand_u ^ INT_MIN

        def count_block(kb, acc):
            blk = key_scr[kb]
            for c in range(tk // LANES):
                acc = acc + jnp.where(blk[:, c * LANES:(c + 1) * LANES] >= cand_s, 1.0, 0.0)
            return acc

        acc = lax.fori_loop(0, n_kb, count_block, jnp.zeros((tq, LANES), jnp.float32))
        cnt = jnp.sum(acc, axis=1, keepdims=True)
        return jnp.where(cnt >= topk, cand_u, t_u)

    t_u = lax.fori_loop(0, 32, bit_pass, jnp.zeros((tq, 1), jnp.int32))
    t_s = jnp.maximum(t_u ^ INT_MIN, INT_MIN + 1)

    for kb in range(n_kb_all):
        @pl.when(kb < n_kb)
        def _():
            mask_o[0, kb] = jnp.where(key_scr[kb] >= t_s, 0.0, NEG_BIG).astype(mask_o.dtype)

        @pl.when(kb >= n_kb)
        def _():
            mask_o[0, kb] = jnp.full((tq, tk), NEG_BIG, mask_o.dtype)


def _indexer(iq3, iklo3, ikhi3, iw3, topk):
    b, s, _ = iq3.shape
    tq = _tile(s, 128)
    tk = _tile(s, KEY_BLOCK)
    return pl.pallas_call(
        functools.partial(_indexer_kernel, tq=tq, tk=tk, topk=topk),
        out_shape=jax.ShapeDtypeStruct((b, s // tk, s, tk), jnp.bfloat16),
        grid=(b, s // tq),
        in_specs=[pl.BlockSpec((1, tq, GROUP_W), lambda bi, i: (bi, i, 0)),
                  pl.BlockSpec((1, s, LANES), lambda bi, i: (bi, 0, 0)),
                  pl.BlockSpec((1, s, LANES), lambda bi, i: (bi, 0, 0)),
                  pl.BlockSpec((1, tq, LANES), lambda bi, i: (bi, i, 0))],
        out_specs=pl.BlockSpec((1, s // tk, tq, tk), lambda bi, i: (bi, 0, i, 0)),
        scratch_shapes=[pltpu.VMEM((s // tk, tq, tk), jnp.int32)],
        compiler_params=_cparams(("arbitrary", "arbitrary")),
        name="indexer",
    )(iq3, iklo3, ikhi3, iw3)


def _softmax_heads(q_ref, k_ref, v_ref, heads, dqk, dv, n_plain, n_kb, tk, bias_fn, s_scr):
    tq = q_ref.shape[1]
    lane_tiles = [slice(c * LANES, (c + 1) * LANES) for c in range(tk // LANES)]

    def scores(masked):
        def body(kb, mxs):
            rows = pl.ds(pl.multiple_of(kb * tk, tk), tk)
            out = []
            for g, h in enumerate(heads):
                s = _dot_nt(q_ref[0, :, h * dqk:(h + 1) * dqk], k_ref[0, rows, h * dqk:(h + 1) * dqk])
                if masked:
                    s = bias_fn(kb, s)
                s_scr[g, kb] = s
                mx = mxs[g]
                for sl in lane_tiles:
                    mx = jnp.maximum(mx, s[:, sl])
                out.append(mx)
            return tuple(out)
        return body

    mxs = tuple(jnp.full((tq, LANES), NEG_BIG, jnp.float32) for _ in heads)
    mxs = lax.fori_loop(0, n_plain, scores(False), mxs)
    mxs = lax.fori_loop(n_plain, n_kb, scores(True), mxs)
    ms = [jnp.max(mx, axis=1, keepdims=True) for mx in mxs]

    def weigh(kb, carry):
        rows = pl.ds(pl.multiple_of(kb * tk, tk), tk)
        out = []
        for g, h in enumerate(heads):
            l, acc = carry[g]
            p = jnp.exp(s_scr[g, kb] - ms[g])
            for sl in lane_tiles:
                l = l + p[:, sl]
            out.append((l, acc + _dot(p, v_ref[0, rows, h * dv:(h + 1) * dv])))
        return tuple(out)

    zero = (jnp.zeros((tq, LANES), jnp.float32), jnp.zeros((tq, dv), jnp.float32))
    res = lax.fori_loop(0, n_kb, weigh, tuple(zero for _ in heads))
    return [acc / jnp.sum(l, axis=1, keepdims=True) for l, acc in res]


def _attn_a_kernel(q_ref, k_ref, v_ref, bias_ref, g_ref, o_ref, o_scr, s_scr, *, tq, tk):
    i = pl.program_id(1)
    n_kb = ((i + 1) * tq + tk - 1) // tk

    def bias_fn(kb, s):
        return s + bias_ref[0, kb].astype(jnp.float32)

    for h0 in range(0, A_HEADS, HEAD_GROUP):
        heads = list(range(h0, h0 + HEAD_GROUP))
        outs = _softmax_heads(q_ref, k_ref, v_ref, heads, A_HEAD_DIM, A_HEAD_DIM, 0, n_kb, tk, bias_fn, s_scr)
        for h, o in zip(heads, outs):
            o_scr[:, h * A_HEAD_DIM:(h + 1) * A_HEAD_DIM] = o
    o_ref[0] = _rms(o_scr[...], g_ref[...]).astype(o_ref.dtype)


def _attn_a(q3, k3, v3, mask4, g):
    b, s, w = q3.shape
    tk = mask4.shape[3]
    tq = _tile(s, 256)
    full = lambda bi, i: (bi, 0, 0)
    return pl.pallas_call(
        functools.partial(_attn_a_kernel, tq=tq, tk=tk),
        out_shape=jax.ShapeDtypeStruct((b, s, w), _MXU_DTYPE),
        grid=(b, s // tq),
        in_specs=[pl.BlockSpec((1, tq, w), lambda bi, i: (bi, i, 0)),
                  pl.BlockSpec((1, s, w), full, pipeline_mode=pl.Buffered(1)),
                  pl.BlockSpec((1, s, w), full, pipeline_mode=pl.Buffered(1)),
                  pl.BlockSpec((1, s // tk, tq, tk), lambda bi, i: (bi, 0, i, 0)),
                  pl.BlockSpec((1, w), lambda bi, i: (0, 0))],
        out_specs=pl.BlockSpec((1, tq, w), lambda bi, i: (bi, i, 0)),
        scratch_shapes=[pltpu.VMEM((tq, w), jnp.float32), pltpu.VMEM((HEAD_GROUP, s // tk, tq, tk), jnp.float32)],
        compiler_params=_cparams(("arbitrary", "arbitrary")),
        name="attn_a",
    )(q3, k3, v3, mask4, g)


def _attn_b_kernel(q_ref, k_ref, v_ref, g_ref, o_ref, o_scr, s_scr, *, tq, tk):
    i = pl.program_id(1)
    n_kb = ((i + 1) * tq + tk - 1) // tk
    n_plain = (i * tq) // tk
    q_chunk = (i * tq + lax.broadcasted_iota(jnp.int32, (tq, tk), 0)) // CHUNK
    col = lax.broadcasted_iota(jnp.int32, (tq, tk), 1)

    def bias_fn(kb, s):
        return jnp.where((kb * tk + col) // CHUNK <= q_chunk, s, NEG_BIG)

    for h0 in range(0, B_HEADS, HEAD_GROUP):
        heads = list(range(h0, h0 + HEAD_GROUP))
        outs = _softmax_heads(q_ref, k_ref, v_ref, heads, QK_PAD, V_HEAD, n_plain, n_kb, tk, bias_fn, s_scr)
        for h, o in zip(heads, outs):
            o_scr[:, h * V_HEAD:(h + 1) * V_HEAD] = o
    o_ref[0] = _rms(o_scr[...], g_ref[...]).astype(o_ref.dtype)


def _attn_b(q3, k3, v3, g):
    b, s, wq = q3.shape
    wv = v3.shape[2]
    tq = _tile(s, 256)
    tk = _tile(s, KEY_BLOCK)
    full = lambda bi, i: (bi, 0, 0)
    return pl.pallas_call(
        functools.partial(_attn_b_kernel, tq=tq, tk=tk),
        out_shape=jax.ShapeDtypeStruct((b, s, wv), _MXU_DTYPE),
        grid=(b, s // tq),
        in_specs=[pl.BlockSpec((1, tq, wq), lambda bi, i: (bi, i, 0)),
                  pl.BlockSpec((1, s, wq), full, pipeline_mode=pl.Buffered(1)),
                  pl.BlockSpec((1, s, wv), full, pipeline_mode=pl.Buffered(1)),
                  pl.BlockSpec((1, wv), lambda bi, i: (0, 0))],
        out_specs=pl.BlockSpec((1, tq, wv), lambda bi, i: (bi, i, 0)),
        scratch_shapes=[pltpu.VMEM((tq, wv), jnp.float32), pltpu.VMEM((HEAD_GROUP, s // tk, tq, tk), jnp.float32)],
        compiler_params=_cparams(("arbitrary", "arbitrary")),
        name="attn_b",
    )(q3, k3, v3, g)


def _outproj_kernel(ma_ref, mb_ref, w_ref, x_ref, g1_ref, sc2_ref, sh2_ref, lg_ref, lb_ref, wrh_ref, wrl_ref, br_ref,
                    x1_o, h2_o, lgt_o, *, alpha, n_experts, n_split):
    rows_per = x_ref.shape[0] // n_split
    lane = lax.broadcasted_iota(jnp.int32, (rows_per, LANES), 1)
    for part in range(n_split):
        r = slice(part * rows_per, (part + 1) * rows_per)
        mix = (jnp.dot(ma_ref[r, :], w_ref[:A_WIDTH, :], preferred_element_type=jnp.float32)
               + jnp.dot(mb_ref[r, :], w_ref[A_WIDTH:, :], preferred_element_type=jnp.float32))
        x1 = _ln_plain(alpha * x_ref[r, :] + g1_ref[0] * mix) * lg_ref[...] + lb_ref[...]
        x1_o[r, :] = x1
        h2 = _ln_plain(x1) * (1.0 + sc2_ref[0]) + sh2_ref[0]
        h2_o[r, :] = h2
        h_hi = h2.astype(_MXU_DTYPE)
        h_lo = (h2 - h_hi.astype(jnp.float32)).astype(_MXU_DTYPE)
        logits = (jnp.dot(h_hi, wrh_ref[...], preferred_element_type=jnp.float32)
                  + jnp.dot(h_lo, wrh_ref[...], preferred_element_type=jnp.float32)
                  + jnp.dot(h_hi, wrl_ref[...], preferred_element_type=jnp.float32)) + br_ref[...]
        lgt_o[r, :] = jnp.where(lane < n_experts, logits, -jnp.inf)


def _outproj(ma, mb, w_out, x2, ada3, lg, lb, wr, br, seq, alpha, n_experts):
    n, d = x2.shape
    tm = _tile(seq, 512)
    n_split = 2 if tm % 32 == 0 else 1
    wr_hi = wr.astype(_MXU_DTYPE)
    wr_lo = (wr - wr_hi.astype(jnp.float32)).astype(_MXU_DTYPE)
    row = lambda i: (i, 0)
    const = lambda a: pl.BlockSpec(a.shape, lambda i: (0,) * a.ndim)
    ada_spec = lambda col: pl.BlockSpec((1, 1, d), lambda i, col=col: (i * tm // seq, 0, col))
    return pl.pallas_call(
        functools.partial(_outproj_kernel, alpha=alpha, n_experts=n_experts, n_split=n_split),
        out_shape=[jax.ShapeDtypeStruct((n, d), jnp.float32),
                   jax.ShapeDtypeStruct((n, d), jnp.float32),
                   jax.ShapeDtypeStruct((n, LANES), jnp.float32)],
        grid=(n // tm,),
        in_specs=[pl.BlockSpec((tm, A_WIDTH), row), pl.BlockSpec((tm, B_WIDTH), row),
                  pl.BlockSpec(w_out.shape, lambda i: (0, 0), pipeline_mode=pl.Buffered(1)),
                  pl.BlockSpec((tm, d), row), ada_spec(2), ada_spec(4), ada_spec(3),
                  const(lg), const(lb), const(wr_hi), const(wr_lo), const(br)],
        out_specs=[pl.BlockSpec((tm, d), row), pl.BlockSpec((tm, d), row), pl.BlockSpec((tm, LANES), row)],
        compiler_params=_cparams(("arbitrary",)),
        name="outproj",
    )(ma, mb, w_out, x2, ada3, ada3, ada3, lg, lb, wr_hi, wr_lo, br)


def _route_kernel(lgt_ref, top_o, gate_o, rank_o, cnt_o, carry):
    tb = lgt_ref.shape[0]

    @pl.when(pl.program_id(0) == 0)
    def _():
        carry[...] = jnp.zeros_like(carry)

    lane = lax.broadcasted_iota(jnp.int32, (tb, LANES), 1)
    lane_f = lane.astype(jnp.float32)
    work = lgt_ref[...]
    vals, idxs, hots = [], [], []
    for _ in range(TOP_K_EXPERTS):
        m = jnp.max(work, axis=1, keepdims=True)
        idx = jnp.min(jnp.where(work == m, lane_f, float(LANES)), axis=1, keepdims=True)
        hot = lane_f == idx
        vals.append(m)
        idxs.append(idx)
        hots.append(hot)
        work = jnp.where(hot, -jnp.inf, work)

    exps = [jnp.exp(v - vals[0]) for v in vals]
    denom = exps[0] + exps[1] + exps[2] + exps[3]

    member = jnp.zeros((tb, LANES), jnp.float32)
    for hot in hots:
        member = member + jnp.where(hot, 1.0, 0.0)
    r = lax.broadcasted_iota(jnp.int32, (tb, tb), 0)
    c = lax.broadcasted_iota(jnp.int32, (tb, tb), 1)
    before = jnp.where(c < r, 1.0, 0.0).astype(jnp.bfloat16)
    prefix = jnp.dot(before, member.astype(jnp.bfloat16), preferred_element_type=jnp.float32) + carry[...]

    top = jnp.zeros((tb, LANES), jnp.int32)
    gate = jnp.zeros((tb, LANES), jnp.float32)
    rank = jnp.zeros((tb, LANES), jnp.int32)
    for k in range(TOP_K_EXPERTS):
        rk = jnp.sum(jnp.where(hots[k], prefix, 0.0), axis=1, keepdims=True)
        top = jnp.where(lane == k, idxs[k].astype(jnp.int32), top)
        gate = jnp.where(lane == k, exps[k] / denom, gate)
        rank = jnp.where(lane == k, rk.astype(jnp.int32), rank)
    top_o[...] = top
    gate_o[...] = gate
    rank_o[...] = rank
    carry[...] = carry[...] + jnp.sum(member, axis=0, keepdims=True)
    cnt_o[...] = jnp.broadcast_to(carry[...], cnt_o.shape).astype(jnp.int32)


def _route(logits):
    n = logits.shape[0]
    tb = _tile(n, 512)
    row = pl.BlockSpec((tb, LANES), lambda i: (i, 0))
    return pl.pallas_call(
        _route_kernel,
        out_shape=[jax.ShapeDtypeStruct((n, LANES), jnp.int32), jax.ShapeDtypeStruct((n, LANES), jnp.float32),
                   jax.ShapeDtypeStruct((n, LANES), jnp.int32), jax.ShapeDtypeStruct((8, LANES), jnp.int32)],
        grid=(n // tb,),
        in_specs=[row],
        out_specs=[row, row, row, pl.BlockSpec((8, LANES), lambda i: (0, 0))],
        scratch_shapes=[pltpu.VMEM((1, LANES), jnp.float32)],
        compiler_params=_cparams(("arbitrary",)),
        name="route",
    )(logits)


def _row_copy(src, s_row, dst, d_row, sem):
    return pltpu.make_async_copy(src.at[pl.ds(s_row, 1), :], dst.at[pl.ds(d_row, 1), :], sem)


def _dispatch_kernel(dest_ref, fill_ref, h_ref, xs_hbm, zeros, sem, zsem, *, td, sub, n_experts):
    base = pl.program_id(0) * td

    @pl.when(pl.program_id(0) == 0)
    def _():
        zeros[...] = jnp.zeros_like(zeros)
        tail0 = fill_ref[2 * n_experts]
        n_tail = fill_ref[2 * n_experts + 1]

        def tail_copy(j):
            rows = pl.ds(pl.multiple_of(tail0 + j * sub, sub), sub)
            return pltpu.make_async_copy(zeros, xs_hbm.at[rows, :], zsem)

        def for_pad_rows(fn):
            def per_expert(e, c):
                first = fill_ref[e]
                lax.fori_loop(0, fill_ref[n_experts + e], lambda r, c2: (fn(first + r), c2)[1], 0)
                return c
            lax.fori_loop(0, n_experts, per_expert, 0)

        for_pad_rows(lambda r: _row_copy(zeros, 0, xs_hbm, r, zsem).start())
        lax.fori_loop(0, n_tail, lambda j, c: (tail_copy(j).start(), c)[1], 0)
        for_pad_rows(lambda r: _row_copy(zeros, 0, xs_hbm, r, zsem).wait())
        lax.fori_loop(0, n_tail, lambda j, c: (tail_copy(j).wait(), c)[1], 0)

    def issue(t, c):
        for k in range(TOP_K_EXPERTS):
            _row_copy(h_ref, t, xs_hbm, dest_ref[(base + t) * TOP_K_EXPERTS + k], sem).start()
        return c

    lax.fori_loop(0, td, issue, 0, unroll=ROW_DMA_UNROLL)
    landed = xs_hbm.at[pl.ds(0, td * TOP_K_EXPERTS), :]
    pltpu.make_async_copy(landed, landed, sem).wait()


def _dispatch(dest_flat, fill, h2, rows, sub, n_experts):
    n, d = h2.shape
    td = _tile(n, 256)
    return pl.pallas_call(
        functools.partial(_dispatch_kernel, td=td, sub=sub, n_experts=n_experts),
        out_shape=jax.ShapeDtypeStruct((rows, d), h2.dtype),
        grid_spec=pltpu.PrefetchScalarGridSpec(
            num_scalar_prefetch=2, grid=(n // td,),
            in_specs=[pl.BlockSpec((td, d), lambda i, dr, fl: (i, 0))],
            out_specs=pl.BlockSpec(memory_space=pl.ANY),
            scratch_shapes=[pltpu.VMEM((sub, d), h2.dtype), pltpu.SemaphoreType.DMA(()),
                            pltpu.SemaphoreType.DMA(())]),
        compiler_params=pltpu.CompilerParams(dimension_semantics=("arbitrary",), has_side_effects=True,
                                             disable_bounds_checks=True),
        name="dispatch",
    )(dest_flat, fill, h2)


def _swiglu_compact(gu):
    tf2 = gu.shape[1]
    nxt = pltpu.roll(gu, tf2 - 1, 1)
    g = jnp.minimum(gu, SWIGLU_LIMIT)
    u = jnp.clip(nxt, -SWIGLU_LIMIT, SWIGLU_LIMIT)
    act = ((u + 1.0) * (g * jax.nn.sigmoid(SWIGLU_ALPHA * g))).astype(_MXU_DTYPE)
    r = lax.broadcasted_iota(jnp.int32, (2 * LANES, LANES), 0)
    c = lax.broadcasted_iota(jnp.int32, (2 * LANES, LANES), 1)
    pick = jnp.where(r == 2 * c, 1.0, 0.0).astype(_MXU_DTYPE)
    parts = [jnp.dot(act[:, t * 2 * LANES:(t + 1) * 2 * LANES], pick, preferred_element_type=jnp.float32)
             for t in range(tf2 // (2 * LANES))]
    return jnp.concatenate(parts, axis=1).astype(_MXU_DTYPE)


def _moe_gemm_kernel(sbe_ref, row0_ref, nsub_ref, xs_hbm, wgu_ref, bgu_ref, wd_ref, bd_ref, y_hbm,
                     xstage, xb, act, wb, wdb, ystage, gu_scr, sem_x, sem_y, *, sub, n_f):
    del sbe_ref
    sb = pl.program_id(0)
    t = pl.program_id(1)
    nsub = nsub_ref[sb]
    row0 = row0_ref[sb]
    tf = wd_ref.shape[1]

    def x_copy(c, slot):
        rows = pl.ds(pl.multiple_of(row0 + c * sub, sub), sub)
        return pltpu.make_async_copy(xs_hbm.at[rows, :], xstage.at[slot], sem_x.at[slot])

    def y_copy(s, slot):
        rows = pl.ds(pl.multiple_of(row0 + s * sub, sub), sub)
        return pltpu.make_async_copy(ystage.at[slot], y_hbm.at[rows, :], sem_y.at[slot])

    @pl.when(jnp.logical_and(sb == 0, t == 0))
    def _():
        xb[...] = jnp.zeros_like(xb)

    @pl.when(jnp.logical_and(nsub > 0, t < n_f))
    def _():
        wb[...] = wgu_ref[0].astype(wb.dtype)
        wdb[pl.ds(pl.multiple_of(t * tf, tf), tf), :] = wd_ref[0].astype(wdb.dtype)
        bias = bgu_ref[0]

        def gate_up(x):
            return jnp.dot(x, wb[...], preferred_element_type=jnp.float32) + bias

        @pl.when(t == 0)
        def _():
            x_copy(0, 0).start()

            def first(c, carry):
                slot = c % 2

                @pl.when(c + 1 < nsub)
                def _():
                    x_copy(c + 1, 1 - slot).start()

                x_copy(c, slot).wait()
                rows = pl.ds(pl.multiple_of(c * sub, sub), sub)
                xv = xstage[slot].astype(xb.dtype)
                xb[rows, :] = xv
                act[0, rows, :] = _swiglu_compact(gate_up(xv))
                return carry

            lax.fori_loop(0, nsub, first, 0)

        @pl.when(t > 0)
        def _():
            n_trips = (nsub + 1) // 2
            trip_rows = lambda p: pl.ds(pl.multiple_of(p * 2 * sub, 2 * sub), 2 * sub)
            gu_scr[...] = gate_up(xb[trip_rows(0), :])

            def up(p, carry):
                prev = gu_scr[...]
                gu_scr[...] = gate_up(xb[trip_rows(p), :])
                act[t, trip_rows(p - 1), :] = _swiglu_compact(prev)
                return carry

            lax.fori_loop(1, n_trips, up, 0)
            act[t, trip_rows(n_trips - 1), :] = _swiglu_compact(gu_scr[...])

    @pl.when(jnp.logical_and(nsub > 0, t == n_f))
    def _():
        bias = bd_ref[0]

        def down(s, carry):
            slot = s % 2
            rows = pl.ds(pl.multiple_of(s * sub, sub), sub)
            a = jnp.concatenate([act[f, rows, :] for f in range(n_f)], axis=1)
            yt = jnp.dot(a, wdb[...], preferred_element_type=jnp.float32) + bias

            @pl.when(s >= 2)
            def _():
                y_copy(s - 2, slot).wait()

            ystage[slot] = yt
            y_copy(s, slot).start()
            return carry

        lax.fori_loop(0, nsub, down, 0)

        @pl.when(nsub >= 2)
        def _():
            y_copy(nsub - 2, nsub % 2).wait()

        y_copy(nsub - 1, (nsub - 1) % 2).wait()


def _moe_gemm(sb_e, sb_row0, sb_nsub, xs, wgu, bgu3, wd, bd3, sub, max_sub):
    e, d, ff2 = wgu.shape
    ff = ff2 // 2
    tf = _tile(ff, 256)
    n_f = ff // tf
    n_sb = sb_e.shape[0]
    rm = (max_sub + 1) // 2 * 2 * sub
    f_idx = lambda sb, t, ns: jnp.where(ns[sb] > 0, jnp.minimum(t, n_f - 1), n_f - 1)
    return pl.pallas_call(
        functools.partial(_moe_gemm_kernel, sub=sub, n_f=n_f),
        out_shape=jax.ShapeDtypeStruct(xs.shape, jnp.float32),
        grid_spec=pltpu.PrefetchScalarGridSpec(
            num_scalar_prefetch=3, grid=(n_sb, n_f + 1),
            in_specs=[pl.BlockSpec(memory_space=pl.ANY),
                      pl.BlockSpec((1, d, 2 * tf), lambda sb, t, se, r0, ns: (se[sb], 0, f_idx(sb, t, ns))),
                      pl.BlockSpec((1, 1, 2 * tf), lambda sb, t, se, r0, ns: (se[sb], 0, f_idx(sb, t, ns))),
                      pl.BlockSpec((1, tf, d), lambda sb, t, se, r0, ns: (se[sb], f_idx(sb, t, ns), 0)),
                      pl.BlockSpec((1, 1, d), lambda sb, t, se, r0, ns: (se[sb], 0, 0))],
            out_specs=pl.BlockSpec(memory_space=pl.ANY),
            scratch_shapes=[pltpu.VMEM((2, sub, d), jnp.float32),
                            pltpu.VMEM((rm, d), _MXU_DTYPE),
                            pltpu.VMEM((n_f, rm, tf), _MXU_DTYPE),
                            pltpu.VMEM((d, 2 * tf), _MXU_DTYPE),
                            pltpu.VMEM((ff, d), _MXU_DTYPE),
                            pltpu.VMEM((2, sub, d), jnp.float32),
                            pltpu.VMEM((2 * sub, 2 * tf), jnp.float32),
                            pltpu.SemaphoreType.DMA((2,)), pltpu.SemaphoreType.DMA((2,))]),
        input_output_aliases={3: 0},
        compiler_params=_cparams(("arbitrary", "arbitrary")),
        name="moe_gemm",
    )(sb_e, sb_row0, sb_nsub, xs, wgu, bgu3, wd, bd3)


def _combine_kernel(dest_ref, y_hbm, gate_ref, x1_ref, g2_ref, lg_ref, lb_ref, o_ref, buf, sem, *, tc, alpha):
    base = pl.program_id(0) * tc

    def issue(t, c):
        for k in range(TOP_K_EXPERTS):
            _row_copy(y_hbm, dest_ref[(base + t) * TOP_K_EXPERTS + k], buf, k * tc + t, sem).start()
        return c

    lax.fori_loop(0, tc, issue, 0, unroll=ROW_DMA_UNROLL)
    pltpu.make_async_copy(y_hbm.at[pl.ds(0, tc * TOP_K_EXPERTS), :], buf, sem).wait()

    gate = gate_ref[...]
    ffn = buf[0:tc, :] * gate[:, 0:1]
    for k in range(1, TOP_K_EXPERTS):
        ffn = ffn + buf[k * tc:(k + 1) * tc, :] * gate[:, k:k + 1]
    o_ref[...] = _ln_plain(alpha * x1_ref[...] + g2_ref[0] * ffn) * lg_ref[...] + lb_ref[...]


def _combine(dest_flat, y, gates, x1, ada3, lg, lb, seq, alpha):
    n, d = x1.shape
    tc = _tile(seq, 256)
    row = lambda i, dr: (i, 0)
    return pl.pallas_call(
        functools.partial(_combine_kernel, tc=tc, alpha=alpha),
        out_shape=jax.ShapeDtypeStruct((n, d), jnp.float32),
        grid_spec=pltpu.PrefetchScalarGridSpec(
            num_scalar_prefetch=1, grid=(n // tc,),
            in_specs=[pl.BlockSpec(memory_space=pl.ANY),
                      pl.BlockSpec((tc, LANES), row), pl.BlockSpec((tc, d), row),
                      pl.BlockSpec((1, 1, d), lambda i, dr: (i * tc // seq, 0, 5)),
                      pl.BlockSpec((1, d), lambda i, dr: (0, 0)), pl.BlockSpec((1, d), lambda i, dr: (0, 0))],
            out_specs=pl.BlockSpec((tc, d), row),
            scratch_shapes=[pltpu.VMEM((TOP_K_EXPERTS * tc, d), jnp.float32),
                            pltpu.SemaphoreType.DMA(())]),
        compiler_params=pltpu.CompilerParams(dimension_semantics=("arbitrary",), vmem_limit_bytes=VMEM_LIMIT,
                                             disable_bounds_checks=True),
        name="combine",
    )(dest_flat, y, gates, x1, ada3, lg, lb)


def _pad_cols(a, width):
    return jnp.pad(a, ((0, 0), (0, width - a.shape[1])))


def _perm_w_in(w_in):
    o = np.cumsum([0, A_WIDTH, A_WIDTH, A_WIDTH, IDX_HEADS * IDX_DIM, IDX_DIM, IDX_HEADS, Q_LORA, KV_LORA, QK_ROPE])
    seg = lambda i: w_in[:, o[i]:o[i + 1]]
    ikw = _pad_cols(jnp.concatenate([seg(4), seg(5)], axis=1), LANES)
    kr = _pad_cols(seg(8), LANES)
    return jnp.concatenate([seg(0), seg(1), seg(2), seg(3), seg(6), seg(7), ikw, kr], axis=1)


def _perm_w_q_up(w):
    w = w.reshape(Q_LORA, B_HEADS, QK_NOPE + QK_ROPE)
    w = jnp.pad(w, ((0, 0), (0, 0), (0, QK_PAD - QK_NOPE - QK_ROPE)))
    return w.reshape(Q_LORA, B_HEADS * QK_PAD)


def _perm_w_kv_up(w):
    w = w.reshape(KV_LORA, B_HEADS, QK_NOPE + V_HEAD)
    return jnp.concatenate([w[:, :, :QK_NOPE].reshape(KV_LORA, -1), w[:, :, QK_NOPE:].reshape(KV_LORA, -1)], axis=1)


MOE_MAX_SUB = 9


def _moe_sub_rows(n_tokens):
    return 256 if n_tokens >= 4096 else 64


def _super_blocks(cnt, sub, n_assign):
    n_experts = cnt.shape[0]
    q = (cnt + sub - 1) // sub
    pad_ends = jnp.cumsum(q * sub)
    pad_starts = pad_ends - q * sub
    nsb_e = (q + MOE_MAX_SUB - 1) // MOE_MAX_SUB
    sb_end = jnp.cumsum(nsb_e)
    sb_start = sb_end - nsb_e
    n_sb = (n_assign // sub + n_experts + MOE_MAX_SUB - 1) // MOE_MAX_SUB + n_experts
    idx = jnp.arange(n_sb, dtype=jnp.int32)
    valid = idx < sb_end[-1]
    e_of = jnp.minimum(jnp.sum((sb_end[None, :] <= idx[:, None]).astype(jnp.int32), axis=1), n_experts - 1)
    j = idx - sb_start[e_of]
    parts = jnp.maximum(nsb_e[e_of], 1)
    base, rem = q[e_of] // parts, q[e_of] % parts
    nsub = jnp.where(valid, base + (j < rem).astype(jnp.int32), 0)
    row0 = jnp.where(valid, pad_starts[e_of] + (j * base + jnp.minimum(j, rem)) * sub, 0)
    e_last = jnp.max(jnp.where(valid, e_of, 0))
    sb_e = jnp.where(valid, e_of, e_last)
    i32 = lambda a: a.astype(jnp.int32)
    rows = (n_assign // sub + n_experts) * sub
    fill = jnp.concatenate([pad_starts + cnt, q * sub - cnt, jnp.stack([pad_ends[-1], (rows - pad_ends[-1]) // sub])])
    return pad_starts, i32(sb_e), i32(row0), i32(nsub), i32(fill)


def kernel(x, c, positions, w_ada, b_ada, w_in, idx_k_norm_g, idx_k_norm_b, q_norm_g, w_q_up, kv_norm_g, w_kv_up,
           out_norm_a_g, out_norm_b_g, w_out, ln_mix_g, ln_mix_b, w_router, b_router, w_gate_up, b_gate_up,
           w_down, b_down, ln_ffn_g, ln_ffn_b):
    bsz, seq, d = x.shape
    depth = w_ada.shape[0]
    n_experts = w_router.shape[2]
    n = bsz * seq
    alpha = (2 * depth) ** 0.25
    topk = min(TOPK_MAX, seq // 4)
    sub = _moe_sub_rows(n)
    n_assign = n * TOP_K_EXPERTS
    rows = (n_assign // sub + n_experts) * sub

    ca, sa, ci, si, cm, sm = _rope_tables(positions.reshape(n, 1))
    c8 = jnp.pad(c, ((0, 8 - bsz), (0, 0)))
    x2 = x.reshape(n, d)
    row = lambda a: a.reshape(1, -1)

    for l in range(depth):
        ada3 = _ada(c8, w_ada[l], row(b_ada[l]))[:bsz].reshape(bsz, 1, 6 * d)

        aq, ak, av, iq, small = _proj(x2, ada3, _perm_w_in(w_in[l]).astype(_MXU_DTYPE), ca, sa, ci, si, seq)
        qb, kb, vb, iklo, ikhi, iw = _mla_prep(
            small, ci, si, cm, sm, row(q_norm_g[l]), row(kv_norm_g[l]),
            _pad_cols(row(idx_k_norm_g[l]), LANES), _pad_cols(row(idx_k_norm_b[l]), LANES),
            _perm_w_q_up(w_q_up[l]).astype(_MXU_DTYPE), _perm_w_kv_up(w_kv_up[l]).astype(_MXU_DTYPE))
        b3 = lambda a: a.reshape(bsz, seq, a.shape[-1])
        mask = _indexer(b3(iq), b3(iklo), b3(ikhi), b3(iw), topk)
        out_a = _attn_a(b3(aq), b3(ak), b3(av), mask, row(out_norm_a_g[l]))
        out_b = _attn_b(b3(qb), b3(kb), b3(vb), row(out_norm_b_g[l]))

        x1, h2, logits = _outproj(
            out_a.reshape(n, A_WIDTH), out_b.reshape(n, B_WIDTH), w_out[l].astype(_MXU_DTYPE), x2, ada3,
            row(ln_mix_g[l]), row(ln_mix_b[l]), _pad_cols(w_router[l], LANES), _pad_cols(row(b_router[l]), LANES),
            seq, alpha, n_experts)
        top, gates, rank, counts = _route(logits)

        pad_starts, sb_e, sb_row0, sb_nsub, fill = _super_blocks(counts[0, :n_experts], sub, n_assign)
        dest = (pad_starts[top[:, :TOP_K_EXPERTS]] + rank[:, :TOP_K_EXPERTS]).reshape(-1).astype(jnp.int32)

        xs = _dispatch(dest, fill, h2, rows, sub, n_experts)
        y = _moe_gemm(sb_e, sb_row0, sb_nsub, xs, w_gate_up[l], b_gate_up[l].reshape(n_experts, 1, -1), w_down[l],
                      b_down[l].reshape(n_experts, 1, -1), sub, MOE_MAX_SUB)
        x2 = _combine(dest, y, gates, x1, ada3, row(ln_ffn_g[l]), row(ln_ffn_b[l]), seq, alpha)

    return x2.reshape(bsz, seq, d)
```

```python
import functools
import math

import jax
import jax.numpy as jnp
import numpy as np
from jax import lax
from jax.experimental import pallas as pl
from jax.experimental.pallas import tpu as pltpu

CHUNK = 64
ROPE_THETA = 500000.0
EPS = 1e-5
A_HEADS = 8
A_HEAD_DIM = 128
A_ROT_DIM = A_HEAD_DIM // 4
IDX_HEADS = 16
IDX_DIM = 64
IDX_ROT_DIM = IDX_DIM // 4
TOPK_MAX = 256
INDEX_SCALE = (IDX_DIM ** -0.5) * (IDX_HEADS ** -0.5)
B_HEADS = 8
Q_LORA = 512
KV_LORA = 256
QK_NOPE = 128
QK_ROPE = 64
V_HEAD = 128
A_WIDTH = A_HEADS * A_HEAD_DIM
B_WIDTH = B_HEADS * V_HEAD
TOP_K_EXPERTS = 4
SWIGLU_LIMIT = 7.0
SWIGLU_ALPHA = 1.702

LANES = 128
QK_PAD = 256
GROUP_W = 1024
VMEM_LIMIT = 56 * 1024 * 1024
NEG_BIG = -1e30
UNCHECKED_PASSES = 22
KEY_BLOCK = 1024
HEAD_GROUP = 4
ROW_DMA_UNROLL = 8
INT_MIN = -2 ** 31

_MXU_DTYPE = jnp.bfloat16


def _cparams(sem, vmem=VMEM_LIMIT):
    return pltpu.CompilerParams(dimension_semantics=sem, vmem_limit_bytes=vmem)


def _tile(n, t):
    t = min(n, t)
    assert n % t == 0, (n, t)
    return t


def _dot(a, b):
    return jnp.dot(a.astype(_MXU_DTYPE), b.astype(_MXU_DTYPE), preferred_element_type=jnp.float32)


def _dot_nt(a, b):
    return lax.dot_general(a.astype(_MXU_DTYPE), b.astype(_MXU_DTYPE), (((1,), (1,)), ((), ())),
                           preferred_element_type=jnp.float32)


def _ln_plain(x):
    mu = jnp.mean(x, axis=-1, keepdims=True)
    d = x - mu
    var = jnp.mean(d * d, axis=-1, keepdims=True)
    return d * lax.rsqrt(var + EPS)


def _rms(x, g):
    return x * lax.rsqrt(jnp.mean(x * x, axis=-1, keepdims=True) + EPS) * g


def _rope128(x, cos, sin, half, group):
    lane = lax.broadcasted_iota(jnp.int32, x.shape, 1) % group
    partner = jnp.where(lane < half, pltpu.roll(x, LANES - half, 1), pltpu.roll(x, half, 1))
    return x * cos + partner * sin


def _ada_kernel(c_ref, w_ref, b_ref, o_ref):
    c = c_ref[...]
    o_ref[...] = _dot(c * jax.nn.sigmoid(c), w_ref[...]) + b_ref[...]


def _ada(c8, w, b):
    d, n = w.shape
    tn = math.gcd(n, 1024)
    return pl.pallas_call(
        _ada_kernel,
        out_shape=jax.ShapeDtypeStruct((8, n), jnp.float32),
        grid=(n // tn,),
        in_specs=[pl.BlockSpec((8, d), lambda j: (0, 0)),
                  pl.BlockSpec((d, tn), lambda j: (0, j)),
                  pl.BlockSpec((1, tn), lambda j: (0, j))],
        out_specs=pl.BlockSpec((8, tn), lambda j: (0, j)),
        compiler_params=_cparams(("arbitrary",)),
        name="ada",
    )(c8, w, b)


def _rope_rows(rot, group):
    half = rot // 2
    inv_freq = ROPE_THETA ** (-jnp.arange(half, dtype=jnp.float32) / half)
    lane = np.arange(LANES) % group
    freq = jnp.where(lane < rot, inv_freq[lane % half], 0.0)
    sign = np.where(lane < half, -1.0, np.where(lane < rot, 1.0, 0.0)).astype(np.float32)
    return freq.reshape(1, LANES).astype(jnp.float32), jnp.asarray(sign).reshape(1, LANES)


def _rope_tables_kernel(pos_ref, fa, sa, fi, si, fm, sm, ca_o, sa_o, ci_o, si_o, cm_o, sm_o):
    pos = pos_ref[...].astype(jnp.float32)
    for f, s, c_o, s_o in ((fa, sa, ca_o, sa_o), (fi, si, ci_o, si_o), (fm, sm, cm_o, sm_o)):
        ang = pos * f[...]
        c_o[...] = jnp.cos(ang)
        s_o[...] = jnp.sin(ang) * s[...]


def _rope_tables(pos_col):
    n = pos_col.shape[0]
    t = _tile(n, 1024)
    rows = (*_rope_rows(A_ROT_DIM, A_HEAD_DIM), *_rope_rows(IDX_ROT_DIM, IDX_DIM), *_rope_rows(QK_ROPE, LANES))
    row_spec = pl.BlockSpec((1, LANES), lambda i: (0, 0))
    tab_spec = pl.BlockSpec((t, LANES), lambda i: (i, 0))
    return pl.pallas_call(
        _rope_tables_kernel,
        out_shape=[jax.ShapeDtypeStruct((n, LANES), jnp.float32)] * 6,
        grid=(n // t,),
        in_specs=[pl.BlockSpec((t, 1), lambda i: (i, 0))] + [row_spec] * 6,
        out_specs=[tab_spec] * 6,
        compiler_params=_cparams(("arbitrary",)),
        name="rope_tables",
    )(pos_col, *rows)


def _proj_kernel(x_ref, sc_ref, sh_ref, w_ref, ca, sa, ci, si, aq_o, ak_o, av_o, iq_o, sm_o):
    h = (_ln_plain(x_ref[...]) * (1.0 + sc_ref[0]) + sh_ref[0]).astype(w_ref.dtype)

    def group(j):
        return jnp.dot(h, w_ref[:, j * GROUP_W:(j + 1) * GROUP_W], preferred_element_type=jnp.float32)

    def rope_heads(acc, o_ref, cos_ref, sin_ref, half, group_w, scale):
        cos, sin = cos_ref[...], sin_ref[...]
        for t in range(GROUP_W // LANES):
            sl = slice(t * LANES, (t + 1) * LANES)
            o_ref[:, sl] = (_rope128(acc[:, sl], cos, sin, half, group_w) * scale).astype(o_ref.dtype)

    rope_heads(group(0), aq_o, ca, sa, A_ROT_DIM // 2, A_HEAD_DIM, A_HEAD_DIM ** -0.5)
    rope_heads(group(1), ak_o, ca, sa, A_ROT_DIM // 2, A_HEAD_DIM, 1.0)
    av_o[...] = group(2).astype(av_o.dtype)
    rope_heads(group(3), iq_o, ci, si, IDX_ROT_DIM // 2, IDX_DIM, 1.0)
    sm_o[...] = group(4)


def _proj(x2, ada3, w_perm, ca, sa, ci, si, seq):
    n, d = x2.shape
    tm = _tile(seq, 512)
    row = lambda i: (i, 0)
    ada_spec = lambda col: pl.BlockSpec((1, 1, d), lambda i, col=col: (i * tm // seq, 0, col))
    big = lambda dt: jax.ShapeDtypeStruct((n, GROUP_W), dt)
    return pl.pallas_call(
        _proj_kernel,
        out_shape=[big(_MXU_DTYPE)] * 4 + [big(jnp.float32)],
        grid=(n // tm,),
        in_specs=[pl.BlockSpec((tm, d), row), ada_spec(1), ada_spec(0),
                  pl.BlockSpec(w_perm.shape, lambda i: (0, 0), pipeline_mode=pl.Buffered(1))]
                 + [pl.BlockSpec((tm, LANES), row)] * 4,
        out_specs=[pl.BlockSpec((tm, GROUP_W), row)] * 5,
        compiler_params=_cparams(("arbitrary",)),
        name="proj",
    )(x2, ada3, ada3, w_perm, ca, sa, ci, si)


def _mla_prep_kernel(sm_ref, ci, si, cm, sm, qg, kvg, ikg, ikb, wq_ref, wkv_ref,
                     qb_o, kb_o, vb_o, iklo_o, ikhi_o, iw_o):
    small = sm_ref[...]
    qd = small[:, :Q_LORA]
    kvd = small[:, Q_LORA:Q_LORA + KV_LORA]
    ikw = small[:, Q_LORA + KV_LORA:Q_LORA + KV_LORA + LANES]
    kr = small[:, Q_LORA + KV_LORA + LANES:]

    lane = lax.broadcasted_iota(jnp.int32, ikw.shape, 1)
    is_k = lane < IDX_DIM
    ik = jnp.where(is_k, ikw, 0.0)
    mu = jnp.sum(ik, axis=-1, keepdims=True) * (1.0 / IDX_DIM)
    dk = jnp.where(is_k, ikw - mu, 0.0)
    var = jnp.sum(dk * dk, axis=-1, keepdims=True) * (1.0 / IDX_DIM)
    ik = dk * lax.rsqrt(var + EPS) * ikg[...] + ikb[...]
    ik = _rope128(ik, ci[...], si[...], IDX_ROT_DIM // 2, IDX_DIM)
    ik = jnp.where(is_k, ik, 0.0)
    iklo_o[...] = ik.astype(iklo_o.dtype)
    ikhi_o[...] = pltpu.roll(ik, IDX_DIM, 1).astype(ikhi_o.dtype)
    iw_o[...] = pltpu.roll(ikw, LANES - IDX_DIM, 1) * INDEX_SCALE

    cos_m, sin_m = cm[...], sm[...]
    q = _dot(_rms(qd, qg[...]), wq_ref[...])
    scale = (QK_NOPE + QK_ROPE) ** -0.5
    for h in range(B_HEADS):
        lo = slice(h * QK_PAD, h * QK_PAD + LANES)
        hi = slice(h * QK_PAD + LANES, (h + 1) * QK_PAD)
        qb_o[:, lo] = (q[:, lo] * scale).astype(qb_o.dtype)
        qb_o[:, hi] = (_rope128(q[:, hi], cos_m, sin_m, QK_ROPE // 2, LANES) * scale).astype(qb_o.dtype)

    kv = _dot(_rms(kvd, kvg[...]), wkv_ref[...])
    krr = _rope128(kr, cos_m, sin_m, QK_ROPE // 2, LANES).astype(kb_o.dtype)
    for h in range(B_HEADS):
        kb_o[:, h * QK_PAD:h * QK_PAD + LANES] = kv[:, h * QK_NOPE:(h + 1) * QK_NOPE].astype(kb_o.dtype)
        kb_o[:, h * QK_PAD + LANES:(h + 1) * QK_PAD] = krr
    vb_o[...] = kv[:, B_HEADS * QK_NOPE:].astype(vb_o.dtype)


def _mla_prep(small, ci, si, cm, sm, qg, kvg, ikg, ikb, wq, wkv):
    n = small.shape[0]
    tm = _tile(n, 512)
    row = lambda i: (i, 0)
    const = lambda a: pl.BlockSpec(a.shape, lambda i: (0, 0))
    tab = pl.BlockSpec((tm, LANES), row)
    wide = B_HEADS * QK_PAD
    return pl.pallas_call(
        _mla_prep_kernel,
        out_shape=[jax.ShapeDtypeStruct((n, wide), _MXU_DTYPE), jax.ShapeDtypeStruct((n, wide), _MXU_DTYPE),
                   jax.ShapeDtypeStruct((n, B_WIDTH), _MXU_DTYPE),
                   jax.ShapeDtypeStruct((n, LANES), _MXU_DTYPE), jax.ShapeDtypeStruct((n, LANES), _MXU_DTYPE),
                   jax.ShapeDtypeStruct((n, LANES), jnp.float32)],
        grid=(n // tm,),
        in_specs=[pl.BlockSpec((tm, GROUP_W), row), tab, tab, tab, tab,
                  const(qg), const(kvg), const(ikg), const(ikb), const(wq), const(wkv)],
        out_specs=[pl.BlockSpec((tm, wide), row), pl.BlockSpec((tm, wide), row), pl.BlockSpec((tm, B_WIDTH), row),
                   tab, tab, tab],
        compiler_params=_cparams(("arbitrary",)),
        name="mla_prep",
    )(small, ci, si, cm, sm, qg, kvg, ikg, ikb, wq, wkv)


def _indexer_kernel(iq_ref, iklo_ref, ikhi_ref, iw_ref, mask_o, key_scr, *, tq, tk, topk):
    i = pl.program_id(1)
    n_kb_all = mask_o.shape[1]
    n_kb = ((i + 1) * tq + tk - 1) // tk
    q_chunk = (i * tq + lax.broadcasted_iota(jnp.int32, (tq, tk), 0)) // CHUNK
    col = lax.broadcasted_iota(jnp.int32, (tq, tk), 1)
    iw = iw_ref[0]

    def score_block(kb, carry):
        klo = iklo_ref[0, pl.ds(kb * tk, tk), :]
        khi = ikhi_ref[0, pl.ds(kb * tk, tk), :]
        sc = jnp.zeros((tq, tk), jnp.float32)
        for p in range(IDX_HEADS // 2):
            qp = iq_ref[0, :, p * LANES:(p + 1) * LANES]
            for half, kk in ((0, klo), (1, khi)):
                hd = 2 * p + half
                sc = sc + jnp.maximum(_dot_nt(qp, kk), 0.0) * iw[:, hd:hd + 1]
        bits = pltpu.bitcast(sc, jnp.int32)
        key = jnp.where(bits < 0, bits ^ 0x7FFFFFFF, bits)
        visible = (kb * tk + col) // CHUNK <= q_chunk
        key_scr[kb] = jnp.where(visible, key, INT_MIN)
        return carry

    lax.fori_loop(0, n_kb, score_block, 0)

    def search_cond(state):
        it, _, settled = state
        return jnp.logical_and(it < 32, jnp.min(settled) == 0)

    def bit_pass(state):
        it, t_u, settled = state
        cand_u = t_u | lax.shift_left(jnp.int32(1), 31 - it)
        cand_s = cand_u ^ INT_MIN

        def count_block(kb, acc):
            blk = key_scr[kb]
            for c in range(tk // LANES):
                acc = acc + jnp.where(blk[:, c * LANES:(c + 1) * LANES] >= cand_s, 1.0, 0.0)
            return acc

        acc = lax.fori_loop(0, n_kb, count_block, jnp.zeros((tq, LANES), jnp.float32))
        cnt = jnp.sum(acc, axis=1, keepdims=True)
        open_row = settled == 0
        t_u = jnp.where(jnp.logical_and(open_row, cnt >= topk), cand_u, t_u)
        settled = jnp.where(jnp.logical_and(open_row, cnt == topk), 1, settled)
        return it + 1, t_u, settled

    state = (jnp.int32(0), jnp.zeros((tq, 1), jnp.int32), jnp.zeros((tq, 1), jnp.int32))
    state = lax.fori_loop(0, UNCHECKED_PASSES, lambda _, st: bit_pass(st), state)
    _, t_u, _ = lax.while_loop(search_cond, bit_pass, state)
    t_s = jnp.maximum(t_u ^ INT_MIN, INT_MIN + 1)

    for kb in range(n_kb_all):
        @pl.when(kb < n_kb)
        def _():
            mask_o[0, kb] = jnp.where(key_scr[kb] >= t_s, 0.0, NEG_BIG).astype(mask_o.dtype)

        @pl.when(kb >= n_kb)
        def _():
            mask_o[0, kb] = jnp.full((tq, tk), NEG_BIG, mask_o.dtype)


def _indexer(iq3, iklo3, ikhi3, iw3, topk):
    b, s, _ = iq3.shape
    tq = _tile(s, 128)
    tk = _tile(s, KEY_BLOCK)
    return pl.pallas_call(
        functools.partial(_indexer_kernel, tq=tq, tk=tk, topk=topk),
        out_shape=jax.ShapeDtypeStruct((b, s // tk, s, tk), jnp.bfloat16),
        grid=(b, s // tq),
        in_specs=[pl.BlockSpec((1, tq, GROUP_W), lambda bi, i: (bi, i, 0)),
                  pl.BlockSpec((1, s, LANES), lambda bi, i: (bi, 0, 0)),
                  pl.BlockSpec((1, s, LANES), lambda bi, i: (bi, 0, 0)),
                  pl.BlockSpec((1, tq, LANES), lambda bi, i: (bi, i, 0))],
        out_specs=pl.BlockSpec((1, s // tk, tq, tk), lambda bi, i: (bi, 0, i, 0)),
        scratch_shapes=[pltpu.VMEM((s // tk, tq, tk), jnp.int32)],
        compiler_params=_cparams(("arbitrary", "arbitrary")),
        name="indexer",
    )(iq3, iklo3, ikhi3, iw3)


def _softmax_heads(q_ref, k_ref, v_ref, heads, dqk, dv, n_plain, n_kb, tk, bias_fn, s_scr):
    tq = q_ref.shape[1]
    lane_tiles = [slice(c * LANES, (c + 1) * LANES) for c in range(tk // LANES)]

    def scores(masked):
        def body(kb, mxs):
            rows = pl.ds(pl.multiple_of(kb * tk, tk), tk)
            out = []
            for g, h in enumerate(heads):
                s = _dot_nt(q_ref[0, :, h * dqk:(h + 1) * dqk], k_ref[0, rows, h * dqk:(h + 1) * dqk])
                if masked:
                    s = bias_fn(kb, s)
                s_scr[g, kb] = s
                mx = mxs[g]
                for sl in lane_tiles:
                    mx = jnp.maximum(mx, s[:, sl])
                out.append(mx)
            return tuple(out)
        return body

    mxs = tuple(jnp.full((tq, LANES), NEG_BIG, jnp.float32) for _ in heads)
    mxs = lax.fori_loop(0, n_plain, scores(False), mxs)
    mxs = lax.fori_loop(n_plain, n_kb, scores(True), mxs)
    ms = [jnp.max(mx, axis=1, keepdims=True) for mx in mxs]

    def weigh(kb, carry):
        rows = pl.ds(pl.multiple_of(kb * tk, tk), tk)
        out = []
        for g, h in enumerate(heads):
            l, acc = carry[g]
            p = jnp.exp(s_scr[g, kb] - ms[g])
            for sl in lane_tiles:
                l = l + p[:, sl]
            out.append((l, acc + _dot(p, v_ref[0, rows, h * dv:(h + 1) * dv])))
        return tuple(out)

    zero = (jnp.zeros((tq, LANES), jnp.float32), jnp.zeros((tq, dv), jnp.float32))
    res = lax.fori_loop(0, n_kb, weigh, tuple(zero for _ in heads))
    return [acc / jnp.sum(l, axis=1, keepdims=True) for l, acc in res]


def _attn_a_kernel(q_ref, k_ref, v_ref, bias_ref, g_ref, o_ref, o_scr, s_scr, *, tq, tk):
    i = pl.program_id(1)
    n_kb = ((i + 1) * tq + tk - 1) // tk

    def bias_fn(kb, s):
        return s + bias_ref[0, kb].astype(jnp.float32)

    for h0 in range(0, A_HEADS, HEAD_GROUP):
        heads = list(range(h0, h0 + HEAD_GROUP))
        outs = _softmax_heads(q_ref, k_ref, v_ref, heads, A_HEAD_DIM, A_HEAD_DIM, 0, n_kb, tk, bias_fn, s_scr)
        for h, o in zip(heads, outs):
            o_scr[:, h * A_HEAD_DIM:(h + 1) * A_HEAD_DIM] = o
    o_ref[0] = _rms(o_scr[...], g_ref[...]).astype(o_ref.dtype)


def _attn_a(q3, k3, v3, mask4, g):
    b, s, w = q3.shape
    tk = mask4.shape[3]
    tq = _tile(s, 256)
    full = lambda bi, i: (bi, 0, 0)
    return pl.pallas_call(
        functools.partial(_attn_a_kernel, tq=tq, tk=tk),
        out_shape=jax.ShapeDtypeStruct((b, s, w), _MXU_DTYPE),
        grid=(b, s // tq),
        in_specs=[pl.BlockSpec((1, tq, w), lambda bi, i: (bi, i, 0)),
                  pl.BlockSpec((1, s, w), full, pipeline_mode=pl.Buffered(1)),
                  pl.BlockSpec((1, s, w), full, pipeline_mode=pl.Buffered(1)),
                  pl.BlockSpec((1, s // tk, tq, tk), lambda bi, i: (bi, 0, i, 0)),
                  pl.BlockSpec((1, w), lambda bi, i: (0, 0))],
        out_specs=pl.BlockSpec((1, tq, w), lambda bi, i: (bi, i, 0)),
        scratch_shapes=[pltpu.VMEM((tq, w), jnp.float32), pltpu.VMEM((HEAD_GROUP, s // tk, tq, tk), jnp.float32)],
        compiler_params=_cparams(("arbitrary", "arbitrary")),
        name="attn_a",
    )(q3, k3, v3, mask4, g)


def _attn_b_kernel(q_ref, k_ref, v_ref, g_ref, o_ref, o_scr, s_scr, *, tq, tk):
    i = pl.program_id(1)
    n_kb = ((i + 1) * tq + tk - 1) // tk
    n_plain = (i * tq) // tk
    q_chunk = (i * tq + lax.broadcasted_iota(jnp.int32, (tq, tk), 0)) // CHUNK
    col = lax.broadcasted_iota(jnp.int32, (tq, tk), 1)

    def bias_fn(kb, s):
        return jnp.where((kb * tk + col) // CHUNK <= q_chunk, s, NEG_BIG)

    for h0 in range(0, B_HEADS, HEAD_GROUP):
        heads = list(range(h0, h0 + HEAD_GROUP))
        outs = _softmax_heads(q_ref, k_ref, v_ref, heads, QK_PAD, V_HEAD, n_plain, n_kb, tk, bias_fn, s_scr)
        for h, o in zip(heads, outs):
            o_scr[:, h * V_HEAD:(h + 1) * V_HEAD] = o
    o_ref[0] = _rms(o_scr[...], g_ref[...]).astype(o_ref.dtype)


def _attn_b(q3, k3, v3, g):
    b, s, wq = q3.shape
    wv = v3.shape[2]
    tq = _tile(s, 256)
    tk = _tile(s, KEY_BLOCK)
    full = lambda bi, i: (bi, 0, 0)
    return pl.pallas_call(
        functools.partial(_attn_b_kernel, tq=tq, tk=tk),
        out_shape=jax.ShapeDtypeStruct((b, s, wv), _MXU_DTYPE),
        grid=(b, s // tq),
        in_specs=[pl.BlockSpec((1, tq, wq), lambda bi, i: (bi, i, 0)),
                  pl.BlockSpec((1, s, wq), full, pipeline_mode=pl.Buffered(1)),
                  pl.BlockSpec((1, s, wv), full, pipeline_mode=pl.Buffered(1)),
                  pl.BlockSpec((1, wv), lambda bi, i: (0, 0))],
        out_specs=pl.BlockSpec((1, tq, wv), lambda bi, i: (bi, i, 0)),
        scratch_shapes=[pltpu.VMEM((tq, wv), jnp.float32), pltpu.VMEM((HEAD_GROUP, s // tk, tq, tk), jnp.float32)],
        compiler_params=_cparams(("arbitrary", "arbitrary")),
        name="attn_b",
    )(q3, k3, v3, g)


def _outproj_kernel(ma_ref, mb_ref, w_ref, x_ref, g1_ref, sc2_ref, sh2_ref, lg_ref, lb_ref, wrh_ref, wrl_ref, br_ref,
                    x1_o, h2_o, lgt_o, *, alpha, n_experts, n_split):
    rows_per = x_ref.shape[0] // n_split
    lane = lax.broadcasted_iota(jnp.int32, (rows_per, LANES), 1)
    for part in range(n_split):
        r = slice(part * rows_per, (part + 1) * rows_per)
        mix = (jnp.dot(ma_ref[r, :], w_ref[:A_WIDTH, :], preferred_element_type=jnp.float32)
               + jnp.dot(mb_ref[r, :], w_ref[A_WIDTH:, :], preferred_element_type=jnp.float32))
        x1 = _ln_plain(alpha * x_ref[r, :] + g1_ref[0] * mix) * lg_ref[...] + lb_ref[...]
        x1_o[r, :] = x1
        h2 = _ln_plain(x1) * (1.0 + sc2_ref[0]) + sh2_ref[0]
        h2_o[r, :] = h2
        h_hi = h2.astype(_MXU_DTYPE)
        h_lo = (h2 - h_hi.astype(jnp.float32)).astype(_MXU_DTYPE)
        logits = (jnp.dot(h_hi, wrh_ref[...], preferred_element_type=jnp.float32)
                  + jnp.dot(h_lo, wrh_ref[...], preferred_element_type=jnp.float32)
                  + jnp.dot(h_hi, wrl_ref[...], preferred_element_type=jnp.float32)) + br_ref[...]
        lgt_o[r, :] = jnp.where(lane < n_experts, logits, -jnp.inf)


def _outproj(ma, mb, w_out, x2, ada3, lg, lb, wr, br, seq, alpha, n_experts):
    n, d = x2.shape
    tm = _tile(seq, 512)
    n_split = 2 if tm % 32 == 0 else 1
    wr_hi = wr.astype(_MXU_DTYPE)
    wr_lo = (wr - wr_hi.astype(jnp.float32)).astype(_MXU_DTYPE)
    row = lambda i: (i, 0)
    const = lambda a: pl.BlockSpec(a.shape, lambda i: (0,) * a.ndim)
    ada_spec = lambda col: pl.BlockSpec((1, 1, d), lambda i, col=col: (i * tm // seq, 0, col))
    return pl.pallas_call(
        functools.partial(_outproj_kernel, alpha=alpha, n_experts=n_experts, n_split=n_split),
        out_shape=[jax.ShapeDtypeStruct((n, d), jnp.float32),
                   jax.ShapeDtypeStruct((n, d), jnp.float32),
                   jax.ShapeDtypeStruct((n, LANES), jnp.float32)],
        grid=(n // tm,),
        in_specs=[pl.BlockSpec((tm, A_WIDTH), row), pl.BlockSpec((tm, B_WIDTH), row),
                  pl.BlockSpec(w_out.shape, lambda i: (0, 0), pipeline_mode=pl.Buffered(1)),
                  pl.BlockSpec((tm, d), row), ada_spec(2), ada_spec(4), ada_spec(3),
                  const(lg), const(lb), const(wr_hi), const(wr_lo), const(br)],
        out_specs=[pl.BlockSpec((tm, d), row), pl.BlockSpec((tm, d), row), pl.BlockSpec((tm, LANES), row)],
        compiler_params=_cparams(("arbitrary",)),
        name="outproj",
    )(ma, mb, w_out, x2, ada3, ada3, ada3, lg, lb, wr_hi, wr_lo, br)


def _route_kernel(lgt_ref, top_o, gate_o, rank_o, cnt_o, carry):
    tb = lgt_ref.shape[0]

    @pl.when(pl.program_id(0) == 0)
    def _():
        carry[...] = jnp.zeros_like(carry)

    lane = lax.broadcasted_iota(jnp.int32, (tb, LANES), 1)
    lane_f = lane.astype(jnp.float32)
    work = lgt_ref[...]
    vals, idxs, hots = [], [], []
    for _ in range(TOP_K_EXPERTS):
        m = jnp.max(work, axis=1, keepdims=True)
        idx = jnp.min(jnp.where(work == m, lane_f, float(LANES)), axis=1, keepdims=True)
        hot = lane_f == idx
        vals.append(m)
        idxs.append(idx)
        hots.append(hot)
        work = jnp.where(hot, -jnp.inf, work)

    exps = [jnp.exp(v - vals[0]) for v in vals]
    denom = exps[0] + exps[1] + exps[2] + exps[3]

    member = jnp.zeros((tb, LANES), jnp.float32)
    for hot in hots:
        member = member + jnp.where(hot, 1.0, 0.0)
    r = lax.broadcasted_iota(jnp.int32, (tb, tb), 0)
    c = lax.broadcasted_iota(jnp.int32, (tb, tb), 1)
    before = jnp.where(c < r, 1.0, 0.0).astype(jnp.bfloat16)
    prefix = jnp.dot(before, member.astype(jnp.bfloat16), preferred_element_type=jnp.float32) + carry[...]

    top = jnp.zeros((tb, LANES), jnp.int32)
    gate = jnp.zeros((tb, LANES), jnp.float32)
    rank = jnp.zeros((tb, LANES), jnp.int32)
    for k in range(TOP_K_EXPERTS):
        rk = jnp.sum(jnp.where(hots[k], prefix, 0.0), axis=1, keepdims=True)
        top = jnp.where(lane == k, idxs[k].astype(jnp.int32), top)
        gate = jnp.where(lane == k, exps[k] / denom, gate)
        rank = jnp.where(lane == k, rk.astype(jnp.int32), rank)
    top_o[...] = top
    gate_o[...] = gate
    rank_o[...] = rank
    carry[...] = carry[...] + jnp.sum(member, axis=0, keepdims=True)
    cnt_o[...] = jnp.broadcast_to(carry[...], cnt_o.shape).astype(jnp.int32)


def _route(logits):
    n = logits.shape[0]
    tb = _tile(n, 512)
    row = pl.BlockSpec((tb, LANES), lambda i: (i, 0))
    return pl.pallas_call(
        _route_kernel,
        out_shape=[jax.ShapeDtypeStruct((n, LANES), jnp.int32), jax.ShapeDtypeStruct((n, LANES), jnp.float32),
                   jax.ShapeDtypeStruct((n, LANES), jnp.int32), jax.ShapeDtypeStruct((8, LANES), jnp.int32)],
        grid=(n // tb,),
        in_specs=[row],
        out_specs=[row, row, row, pl.BlockSpec((8, LANES), lambda i: (0, 0))],
        scratch_shapes=[pltpu.VMEM((1, LANES), jnp.float32)],
        compiler_params=_cparams(("arbitrary",)),
        name="route",
    )(logits)


def _row_copy(src, s_row, dst, d_row, sem):
    return pltpu.make_async_copy(src.at[pl.ds(s_row, 1), :], dst.at[pl.ds(d_row, 1), :], sem)


def _dispatch_kernel(dest_ref, fill_ref, h_ref, xs_hbm, zeros, sem, zsem, *, td, sub, n_experts):
    base = pl.program_id(0) * td

    @pl.when(pl.program_id(0) == 0)
    def _():
        zeros[...] = jnp.zeros_like(zeros)
        tail0 = fill_ref[2 * n_experts]
        n_tail = fill_ref[2 * n_experts + 1]

        def tail_copy(j):
            rows = pl.ds(pl.multiple_of(tail0 + j * sub, sub), sub)
            return pltpu.make_async_copy(zeros, xs_hbm.at[rows, :], zsem)

        def for_pad_rows(fn):
            def per_expert(e, c):
                first = fill_ref[e]
                lax.fori_loop(0, fill_ref[n_experts + e], lambda r, c2: (fn(first + r), c2)[1], 0)
                return c
            lax.fori_loop(0, n_experts, per_expert, 0)

        for_pad_rows(lambda r: _row_copy(zeros, 0, xs_hbm, r, zsem).start())
        lax.fori_loop(0, n_tail, lambda j, c: (tail_copy(j).start(), c)[1], 0)
        for_pad_rows(lambda r: _row_copy(zeros, 0, xs_hbm, r, zsem).wait())
        lax.fori_loop(0, n_tail, lambda j, c: (tail_copy(j).wait(), c)[1], 0)

    def issue(t, c):
        for k in range(TOP_K_EXPERTS):
            _row_copy(h_ref, t, xs_hbm, dest_ref[(base + t) * TOP_K_EXPERTS + k], sem).start()
        return c

    lax.fori_loop(0, td, issue, 0, unroll=ROW_DMA_UNROLL)
    landed = xs_hbm.at[pl.ds(0, td * TOP_K_EXPERTS), :]
    pltpu.make_async_copy(landed, landed, sem).wait()


def _dispatch(dest_flat, fill, h2, rows, sub, n_experts):
    n, d = h2.shape
    td = _tile(n, 256)
    return pl.pallas_call(
        functools.partial(_dispatch_kernel, td=td, sub=sub, n_experts=n_experts),
        out_shape=jax.ShapeDtypeStruct((rows, d), h2.dtype),
        grid_spec=pltpu.PrefetchScalarGridSpec(
            num_scalar_prefetch=2, grid=(n // td,),
            in_specs=[pl.BlockSpec((td, d), lambda i, dr, fl: (i, 0))],
            out_specs=pl.BlockSpec(memory_space=pl.ANY),
            scratch_shapes=[pltpu.VMEM((sub, d), h2.dtype), pltpu.SemaphoreType.DMA(()),
                            pltpu.SemaphoreType.DMA(())]),
        compiler_params=pltpu.CompilerParams(dimension_semantics=("arbitrary",), has_side_effects=True,
                                             disable_bounds_checks=True),
        name="dispatch",
    )(dest_flat, fill, h2)


def _swiglu_compact(gu):
    tf2 = gu.shape[1]
    nxt = pltpu.roll(gu, tf2 - 1, 1)
    g = jnp.minimum(gu, SWIGLU_LIMIT)
    u = jnp.clip(nxt, -SWIGLU_LIMIT, SWIGLU_LIMIT)
    act = ((u + 1.0) * (g * jax.nn.sigmoid(SWIGLU_ALPHA * g))).astype(_MXU_DTYPE)
    r = lax.broadcasted_iota(jnp.int32, (2 * LANES, LANES), 0)
    c = lax.broadcasted_iota(jnp.int32, (2 * LANES, LANES), 1)
    pick = jnp.where(r == 2 * c, 1.0, 0.0).astype(_MXU_DTYPE)
    parts = [jnp.dot(act[:, t * 2 * LANES:(t + 1) * 2 * LANES], pick, preferred_element_type=jnp.float32)
             for t in range(tf2 // (2 * LANES))]
    return jnp.concatenate(parts, axis=1).astype(_MXU_DTYPE)


def _moe_gemm_kernel(sbe_ref, row0_ref, nsub_ref, xs_hbm, wgu_ref, bgu_ref, wd_ref, bd_ref, y_hbm,
                     xstage, xb, act, wb, wdb, ystage, gu_scr, sem_x, sem_y, *, sub, n_f):
    del sbe_ref
    sb = pl.program_id(0)
    t = pl.program_id(1)
    nsub = nsub_ref[sb]
    row0 = row0_ref[sb]
    tf = wd_ref.shape[1]

    def x_copy(c, slot):
        rows = pl.ds(pl.multiple_of(row0 + c * sub, sub), sub)
        return pltpu.make_async_copy(xs_hbm.at[rows, :], xstage.at[slot], sem_x.at[slot])

    def y_copy(s, slot):
        rows = pl.ds(pl.multiple_of(row0 + s * sub, sub), sub)
        return pltpu.make_async_copy(ystage.at[slot], y_hbm.at[rows, :], sem_y.at[slot])

    @pl.when(jnp.logical_and(nsub > 0, t < n_f))
    def _():
        wb[...] = wgu_ref[0].astype(wb.dtype)
        wdb[pl.ds(pl.multiple_of(t * tf, tf), tf), :] = wd_ref[0].astype(wdb.dtype)
        bias = bgu_ref[0]

        def gate_up(x):
            return jnp.dot(x, wb[...], preferred_element_type=jnp.float32) + bias

        @pl.when(t == 0)
        def _():
            x_copy(0, 0).start()

            def first(c, carry):
                slot = c % 2

                @pl.when(c + 1 < nsub)
                def _():
                    x_copy(c + 1, 1 - slot).start()

                x_copy(c, slot).wait()
                rows = pl.ds(pl.multiple_of(c * sub, sub), sub)
                xv = xstage[slot].astype(xb.dtype)
                xb[rows, :] = xv
                act[0, rows, :] = _swiglu_compact(gate_up(xv))
                return carry

            lax.fori_loop(0, nsub, first, 0)

        @pl.when(t > 0)
        def _():
            n_pairs = nsub // 2
            pair_rows = lambda p: pl.ds(pl.multiple_of(p * 2 * sub, 2 * sub), 2 * sub)

            @pl.when(n_pairs > 0)
            def _():
                gu_scr[...] = gate_up(xb[pair_rows(0), :])

                def up(p, carry):
                    prev = gu_scr[...]
                    gu_scr[...] = gate_up(xb[pair_rows(p), :])
                    act[t, pair_rows(p - 1), :] = _swiglu_compact(prev)
                    return carry

                lax.fori_loop(1, n_pairs, up, 0)
                act[t, pair_rows(n_pairs - 1), :] = _swiglu_compact(gu_scr[...])

            @pl.when(nsub % 2 == 1)
            def _():
                rows = pl.ds(pl.multiple_of((nsub - 1) * sub, sub), sub)
                act[t, rows, :] = _swiglu_compact(gate_up(xb[rows, :]))

    @pl.when(jnp.logical_and(nsub > 0, t == n_f))
    def _():
        bias = bd_ref[0]

        def down(s, carry):
            slot = s % 2
            rows = pl.ds(pl.multiple_of(s * sub, sub), sub)
            a = jnp.concatenate([act[f, rows, :] for f in range(n_f)], axis=1)
            yt = jnp.dot(a, wdb[...], preferred_element_type=jnp.float32) + bias

            @pl.when(s >= 2)
            def _():
                y_copy(s - 2, slot).wait()

            ystage[slot] = yt
            y_copy(s, slot).start()
            return carry

        lax.fori_loop(0, nsub, down, 0)

        @pl.when(nsub >= 2)
        def _():
            y_copy(nsub - 2, nsub % 2).wait()

        y_copy(nsub - 1, (nsub - 1) % 2).wait()


def _moe_gemm(sb_e, sb_row0, sb_nsub, xs, wgu, bgu3, wd, bd3, sub, max_sub):
    e, d, ff2 = wgu.shape
    ff = ff2 // 2
    tf = _tile(ff, 256)
    n_f = ff // tf
    n_sb = sb_e.shape[0]
    rm = max_sub * sub
    f_idx = lambda sb, t, ns: jnp.where(ns[sb] > 0, jnp.minimum(t, n_f - 1), n_f - 1)
    return pl.pallas_call(
        functools.partial(_moe_gemm_kernel, sub=sub, n_f=n_f),
        out_shape=jax.ShapeDtypeStruct(xs.shape, jnp.float32),
        grid_spec=pltpu.PrefetchScalarGridSpec(
            num_scalar_prefetch=3, grid=(n_sb, n_f + 1),
            in_specs=[pl.BlockSpec(memory_space=pl.ANY),
                      pl.BlockSpec((1, d, 2 * tf), lambda sb, t, se, r0, ns: (se[sb], 0, f_idx(sb, t, ns))),
                      pl.BlockSpec((1, 1, 2 * tf), lambda sb, t, se, r0, ns: (se[sb], 0, f_idx(sb, t, ns))),
                      pl.BlockSpec((1, tf, d), lambda sb, t, se, r0, ns: (se[sb], f_idx(sb, t, ns), 0)),
                      pl.BlockSpec((1, 1, d), lambda sb, t, se, r0, ns: (se[sb], 0, 0))],
            out_specs=pl.BlockSpec(memory_space=pl.ANY),
            scratch_shapes=[pltpu.VMEM((2, sub, d), jnp.float32),
                            pltpu.VMEM((rm, d), _MXU_DTYPE),
                            pltpu.VMEM((n_f, rm, tf), _MXU_DTYPE),
                            pltpu.VMEM((d, 2 * tf), _MXU_DTYPE),
                            pltpu.VMEM((ff, d), _MXU_DTYPE),
                            pltpu.VMEM((2, sub, d), jnp.float32),
                            pltpu.VMEM((2 * sub, 2 * tf), jnp.float32),
                            pltpu.SemaphoreType.DMA((2,)), pltpu.SemaphoreType.DMA((2,))]),
        input_output_aliases={3: 0},
        compiler_params=_cparams(("arbitrary", "arbitrary")),
        name="moe_gemm",
    )(sb_e, sb_row0, sb_nsub, xs, wgu, bgu3, wd, bd3)


def _combine_kernel(dest_ref, y_hbm, gate_ref, x1_ref, g2_ref, lg_ref, lb_ref, o_ref, buf, sem, *, tc, alpha):
    base = pl.program_id(0) * tc

    def issue(t, c):
        for k in range(TOP_K_EXPERTS):
            _row_copy(y_hbm, dest_ref[(base + t) * TOP_K_EXPERTS + k], buf, k * tc + t, sem).start()
        return c

    lax.fori_loop(0, tc, issue, 0, unroll=ROW_DMA_UNROLL)
    pltpu.make_async_copy(y_hbm.at[pl.ds(0, tc * TOP_K_EXPERTS), :], buf, sem).wait()

    gate = gate_ref[...]
    ffn = buf[0:tc, :] * gate[:, 0:1]
    for k in range(1, TOP_K_EXPERTS):
        ffn = ffn + buf[k * tc:(k + 1) * tc, :] * gate[:, k:k + 1]
    o_ref[...] = _ln_plain(alpha * x1_ref[...] + g2_ref[0] * ffn) * lg_ref[...] + lb_ref[...]


def _combine(dest_flat, y, gates, x1, ada3, lg, lb, seq, alpha):
    n, d = x1.shape
    tc = _tile(seq, 256)
    row = lambda i, dr: (i, 0)
    return pl.pallas_call(
        functools.partial(_combine_kernel, tc=tc, alpha=alpha),
        out_shape=jax.ShapeDtypeStruct((n, d), jnp.float32),
        grid_spec=pltpu.PrefetchScalarGridSpec(
            num_scalar_prefetch=1, grid=(n // tc,),
            in_specs=[pl.BlockSpec(memory_space=pl.ANY),
                      pl.BlockSpec((tc, LANES), row), pl.BlockSpec((tc, d), row),
                      pl.BlockSpec((1, 1, d), lambda i, dr: (i * tc // seq, 0, 5)),
                      pl.BlockSpec((1, d), lambda i, dr: (0, 0)), pl.BlockSpec((1, d), lambda i, dr: (0, 0))],
            out_specs=pl.BlockSpec((tc, d), row),
            scratch_shapes=[pltpu.VMEM((TOP_K_EXPERTS * tc, d), jnp.float32),
                            pltpu.SemaphoreType.DMA(())]),
        compiler_params=pltpu.CompilerParams(dimension_semantics=("arbitrary",), vmem_limit_bytes=VMEM_LIMIT,
                                             disable_bounds_checks=True),
        name="combine",
    )(dest_flat, y, gates, x1, ada3, lg, lb)


def _pad_cols(a, width):
    return jnp.pad(a, ((0, 0), (0, width - a.shape[1])))


def _perm_w_in(w_in):
    o = np.cumsum([0, A_WIDTH, A_WIDTH, A_WIDTH, IDX_HEADS * IDX_DIM, IDX_DIM, IDX_HEADS, Q_LORA, KV_LORA, QK_ROPE])
    seg = lambda i: w_in[:, o[i]:o[i + 1]]
    ikw = _pad_cols(jnp.concatenate([seg(4), seg(5)], axis=1), LANES)
    kr = _pad_cols(seg(8), LANES)
    return jnp.concatenate([seg(0), seg(1), seg(2), seg(3), seg(6), seg(7), ikw, kr], axis=1)


def _perm_w_q_up(w):
    w = w.reshape(Q_LORA, B_HEADS, QK_NOPE + QK_ROPE)
    w = jnp.pad(w, ((0, 0), (0, 0), (0, QK_PAD - QK_NOPE - QK_ROPE)))
    return w.reshape(Q_LORA, B_HEADS * QK_PAD)


def _perm_w_kv_up(w):
    w = w.reshape(KV_LORA, B_HEADS, QK_NOPE + V_HEAD)
    return jnp.concatenate([w[:, :, :QK_NOPE].reshape(KV_LORA, -1), w[:, :, QK_NOPE:].reshape(KV_LORA, -1)], axis=1)


MOE_MAX_SUB = 9


def _moe_sub_rows(n_tokens):
    return 256 if n_tokens >= 4096 else 64


def _super_blocks(cnt, sub, n_assign):
    n_experts = cnt.shape[0]
    q = (cnt + sub - 1) // sub
    pad_ends = jnp.cumsum(q * sub)
    pad_starts = pad_ends - q * sub
    nsb_e = (q + MOE_MAX_SUB - 1) // MOE_MAX_SUB
    sb_end = jnp.cumsum(nsb_e)
    sb_start = sb_end - nsb_e
    n_sb = (n_assign // sub + n_experts + MOE_MAX_SUB - 1) // MOE_MAX_SUB + n_experts
    idx = jnp.arange(n_sb, dtype=jnp.int32)
    valid = idx < sb_end[-1]
    e_of = jnp.minimum(jnp.sum((sb_end[None, :] <= idx[:, None]).astype(jnp.int32), axis=1), n_experts - 1)
    j = idx - sb_start[e_of]
    parts = jnp.maximum(nsb_e[e_of], 1)
    base, rem = q[e_of] // parts, q[e_of] % parts
    nsub = jnp.where(valid, base + (j < rem).astype(jnp.int32), 0)
    row0 = jnp.where(valid, pad_starts[e_of] + (j * base + jnp.minimum(j, rem)) * sub, 0)
    e_last = jnp.max(jnp.where(valid, e_of, 0))
    sb_e = jnp.where(valid, e_of, e_last)
    i32 = lambda a: a.astype(jnp.int32)
    rows = (n_assign // sub + n_experts) * sub
    fill = jnp.concatenate([pad_starts + cnt, q * sub - cnt, jnp.stack([pad_ends[-1], (rows - pad_ends[-1]) // sub])])
    return pad_starts, i32(sb_e), i32(row0), i32(nsub), i32(fill)


def kernel(x, c, positions, w_ada, b_ada, w_in, idx_k_norm_g, idx_k_norm_b, q_norm_g, w_q_up, kv_norm_g, w_kv_up,
           out_norm_a_g, out_norm_b_g, w_out, ln_mix_g, ln_mix_b, w_router, b_router, w_gate_up, b_gate_up,
           w_down, b_down, ln_ffn_g, ln_ffn_b):
    bsz, seq, d = x.shape
    depth = w_ada.shape[0]
    n_experts = w_router.shape[2]
    n = bsz * seq
    alpha = (2 * depth) ** 0.25
    topk = min(TOPK_MAX, seq // 4)
    sub = _moe_sub_rows(n)
    n_assign = n * TOP_K_EXPERTS
    rows = (n_assign // sub + n_experts) * sub

    ca, sa, ci, si, cm, sm = _rope_tables(positions.reshape(n, 1))
    c8 = jnp.pad(c, ((0, 8 - bsz), (0, 0)))
    x2 = x.reshape(n, d)
    row = lambda a: a.reshape(1, -1)

    for l in range(depth):
        ada3 = _ada(c8, w_ada[l], row(b_ada[l]))[:bsz].reshape(bsz, 1, 6 * d)

        aq, ak, av, iq, small = _proj(x2, ada3, _perm_w_in(w_in[l]).astype(_MXU_DTYPE), ca, sa, ci, si, seq)
        qb, kb, vb, iklo, ikhi, iw = _mla_prep(
            small, ci, si, cm, sm, row(q_norm_g[l]), row(kv_norm_g[l]),
            _pad_cols(row(idx_k_norm_g[l]), LANES), _pad_cols(row(idx_k_norm_b[l]), LANES),
            _perm_w_q_up(w_q_up[l]).astype(_MXU_DTYPE), _perm_w_kv_up(w_kv_up[l]).astype(_MXU_DTYPE))
        b3 = lambda a: a.reshape(bsz, seq, a.shape[-1])
        mask = _indexer(b3(iq), b3(iklo), b3(ikhi), b3(iw), topk)
        out_a = _attn_a(b3(aq), b3(ak), b3(av), mask, row(out_norm_a_g[l]))
        out_b = _attn_b(b3(qb), b3(kb), b3(vb), row(out_norm_b_g[l]))

        x1, h2, logits = _outproj(
            out_a.reshape(n, A_WIDTH), out_b.reshape(n, B_WIDTH), w_out[l].astype(_MXU_DTYPE), x2, ada3,
            row(ln_mix_g[l]), row(ln_mix_b[l]), _pad_cols(w_router[l], LANES), _pad_cols(row(b_router[l]), LANES),
            seq, alpha, n_experts)
        top, gates, rank, counts = _route(logits)

        pad_starts, sb_e, sb_row0, sb_nsub, fill = _super_blocks(counts[0, :n_experts], sub, n_assign)
        dest = (pad_starts[top[:, :TOP_K_EXPERTS]] + rank[:, :TOP_K_EXPERTS]).reshape(-1).astype(jnp.int32)

        xs = _dispatch(dest, fill, h2, rows, sub, n_experts)
        y = _moe_gemm(sb_e, sb_row0, sb_nsub, xs, w_gate_up[l], b_gate_up[l].reshape(n_experts, 1, -1), w_down[l],
                      b_down[l].reshape(n_experts, 1, -1), sub, MOE_MAX_SUB)
        x2 = _combine(dest, y, gates, x1, ada3, row(ln_ffn_g[l]), row(ln_ffn_b[l]), seq, alpha)

    return x2.reshape(bsz, seq, d)
```

```python
import functools
import math

import jax
import jax.numpy as jnp
import numpy as np
from jax import lax
from jax.experimental import pallas as pl
from jax.experimental.pallas import tpu as pltpu

CHUNK = 64
ROPE_THETA = 500000.0
EPS = 1e-5
A_HEADS = 8
A_HEAD_DIM = 128
A_ROT_DIM = A_HEAD_DIM // 4
IDX_HEADS = 16
IDX_DIM = 64
IDX_ROT_DIM = IDX_DIM // 4
TOPK_MAX = 256
INDEX_SCALE = (IDX_DIM ** -0.5) * (IDX_HEADS ** -0.5)
B_HEADS = 8
Q_LORA = 512
KV_LORA = 256
QK_NOPE = 128
QK_ROPE = 64
V_HEAD = 128
A_WIDTH = A_HEADS * A_HEAD_DIM
B_WIDTH = B_HEADS * V_HEAD
TOP_K_EXPERTS = 4
SWIGLU_LIMIT = 7.0
SWIGLU_ALPHA = 1.702

LANES = 128
QK_PAD = 256
GROUP_W = 1024
VMEM_LIMIT = 56 * 1024 * 1024
NEG_BIG = -1e30
UNCHECKED_PASSES = 22
KEY_BLOCK = 1024
HEAD_GROUP = 4
ROW_DMA_UNROLL = 8
INT_MIN = -2 ** 31

_MXU_DTYPE = jnp.bfloat16


def _cparams(sem, vmem=VMEM_LIMIT):
    return pltpu.CompilerParams(dimension_semantics=sem, vmem_limit_bytes=vmem)


def _tile(n, t):
    t = min(n, t)
    assert n % t == 0, (n, t)
    return t


def _dot(a, b):
    return jnp.dot(a.astype(_MXU_DTYPE), b.astype(_MXU_DTYPE), preferred_element_type=jnp.float32)


def _dot_nt(a, b):
    return lax.dot_general(a.astype(_MXU_DTYPE), b.astype(_MXU_DTYPE), (((1,), (1,)), ((), ())),
                           preferred_element_type=jnp.float32)


def _ln_plain(x):
    mu = jnp.mean(x, axis=-1, keepdims=True)
    d = x - mu
    var = jnp.mean(d * d, axis=-1, keepdims=True)
    return d * lax.rsqrt(var + EPS)


def _rms(x, g):
    return x * lax.rsqrt(jnp.mean(x * x, axis=-1, keepdims=True) + EPS) * g


def _rope128(x, cos, sin, half, group):
    lane = lax.broadcasted_iota(jnp.int32, x.shape, 1) % group
    partner = jnp.where(lane < half, pltpu.roll(x, LANES - half, 1), pltpu.roll(x, half, 1))
    return x * cos + partner * sin


def _ada_kernel(c_ref, w_ref, b_ref, o_ref):
    c = c_ref[...]
    o_ref[...] = _dot(c * jax.nn.sigmoid(c), w_ref[...]) + b_ref[...]


def _ada(c8, w, b):
    d, n = w.shape
    tn = math.gcd(n, 1024)
    return pl.pallas_call(
        _ada_kernel,
        out_shape=jax.ShapeDtypeStruct((8, n), jnp.float32),
        grid=(n // tn,),
        in_specs=[pl.BlockSpec((8, d), lambda j: (0, 0)),
                  pl.BlockSpec((d, tn), lambda j: (0, j)),
                  pl.BlockSpec((1, tn), lambda j: (0, j))],
        out_specs=pl.BlockSpec((8, tn), lambda j: (0, j)),
        compiler_params=_cparams(("arbitrary",)),
        name="ada",
    )(c8, w, b)


def _rope_rows(rot, group):
    half = rot // 2
    inv_freq = ROPE_THETA ** (-jnp.arange(half, dtype=jnp.float32) / half)
    lane = np.arange(LANES) % group
    freq = jnp.where(lane < rot, inv_freq[lane % half], 0.0)
    sign = np.where(lane < half, -1.0, np.where(lane < rot, 1.0, 0.0)).astype(np.float32)
    return freq.reshape(1, LANES).astype(jnp.float32), jnp.asarray(sign).reshape(1, LANES)


def _rope_tables_kernel(pos_ref, fa, sa, fi, si, fm, sm, ca_o, sa_o, ci_o, si_o, cm_o, sm_o):
    pos = pos_ref[...].astype(jnp.float32)
    for f, s, c_o, s_o in ((fa, sa, ca_o, sa_o), (fi, si, ci_o, si_o), (fm, sm, cm_o, sm_o)):
        ang = pos * f[...]
        c_o[...] = jnp.cos(ang)
        s_o[...] = jnp.sin(ang) * s[...]


def _rope_tables(pos_col):
    n = pos_col.shape[0]
    t = _tile(n, 1024)
    rows = (*_rope_rows(A_ROT_DIM, A_HEAD_DIM), *_rope_rows(IDX_ROT_DIM, IDX_DIM), *_rope_rows(QK_ROPE, LANES))
    row_spec = pl.BlockSpec((1, LANES), lambda i: (0, 0))
    tab_spec = pl.BlockSpec((t, LANES), lambda i: (i, 0))
    return pl.pallas_call(
        _rope_tables_kernel,
        out_shape=[jax.ShapeDtypeStruct((n, LANES), jnp.float32)] * 6,
        grid=(n // t,),
        in_specs=[pl.BlockSpec((t, 1), lambda i: (i, 0))] + [row_spec] * 6,
        out_specs=[tab_spec] * 6,
        compiler_params=_cparams(("arbitrary",)),
        name="rope_tables",
    )(pos_col, *rows)


def _proj_kernel(x_ref, sc_ref, sh_ref, w_ref, ca, sa, ci, si, aq_o, ak_o, av_o, iq_o, sm_o):
    h = (_ln_plain(x_ref[...]) * (1.0 + sc_ref[0]) + sh_ref[0]).astype(w_ref.dtype)

    def group(j):
        return jnp.dot(h, w_ref[:, j * GROUP_W:(j + 1) * GROUP_W], preferred_element_type=jnp.float32)

    def rope_heads(acc, o_ref, cos_ref, sin_ref, half, group_w, scale):
        cos, sin = cos_ref[...], sin_ref[...]
        for t in range(GROUP_W // LANES):
            sl = slice(t * LANES, (t + 1) * LANES)
            o_ref[:, sl] = (_rope128(acc[:, sl], cos, sin, half, group_w) * scale).astype(o_ref.dtype)

    rope_heads(group(0), aq_o, ca, sa, A_ROT_DIM // 2, A_HEAD_DIM, A_HEAD_DIM ** -0.5)
    rope_heads(group(1), ak_o, ca, sa, A_ROT_DIM // 2, A_HEAD_DIM, 1.0)
    av_o[...] = group(2).astype(av_o.dtype)
    rope_heads(group(3), iq_o, ci, si, IDX_ROT_DIM // 2, IDX_DIM, 1.0)
    sm_o[...] = group(4)


def _proj(x2, ada3, w_perm, ca, sa, ci, si, seq):
    n, d = x2.shape
    tm = _tile(seq, 512)
    row = lambda i: (i, 0)
    ada_spec = lambda col: pl.BlockSpec((1, 1, d), lambda i, col=col: (i * tm // seq, 0, col))
    big = lambda dt: jax.ShapeDtypeStruct((n, GROUP_W), dt)
    return pl.pallas_call(
        _proj_kernel,
        out_shape=[big(_MXU_DTYPE)] * 4 + [big(jnp.float32)],
        grid=(n // tm,),
        in_specs=[pl.BlockSpec((tm, d), row), ada_spec(1), ada_spec(0),
                  pl.BlockSpec(w_perm.shape, lambda i: (0, 0), pipeline_mode=pl.Buffered(1))]
                 + [pl.BlockSpec((tm, LANES), row)] * 4,
        out_specs=[pl.BlockSpec((tm, GROUP_W), row)] * 5,
        compiler_params=_cparams(("arbitrary",)),
        name="proj",
    )(x2, ada3, ada3, w_perm, ca, sa, ci, si)


def _mla_prep_kernel(sm_ref, ci, si, cm, sm, qg, kvg, ikg, ikb, wq_ref, wkv_ref,
                     qb_o, kb_o, vb_o, iklo_o, ikhi_o, iw_o):
    small = sm_ref[...]
    qd = small[:, :Q_LORA]
    kvd = small[:, Q_LORA:Q_LORA + KV_LORA]
    ikw = small[:, Q_LORA + KV_LORA:Q_LORA + KV_LORA + LANES]
    kr = small[:, Q_LORA + KV_LORA + LANES:]

    lane = lax.broadcasted_iota(jnp.int32, ikw.shape, 1)
    is_k = lane < IDX_DIM
    ik = jnp.where(is_k, ikw, 0.0)
    mu = jnp.sum(ik, axis=-1, keepdims=True) * (1.0 / IDX_DIM)
    dk = jnp.where(is_k, ikw - mu, 0.0)
    var = jnp.sum(dk * dk, axis=-1, keepdims=True) * (1.0 / IDX_DIM)
    ik = dk * lax.rsqrt(var + EPS) * ikg[...] + ikb[...]
    ik = _rope128(ik, ci[...], si[...], IDX_ROT_DIM // 2, IDX_DIM)
    ik = jnp.where(is_k, ik, 0.0)
    iklo_o[...] = ik.astype(iklo_o.dtype)
    ikhi_o[...] = pltpu.roll(ik, IDX_DIM, 1).astype(ikhi_o.dtype)
    iw_o[...] = pltpu.roll(ikw, LANES - IDX_DIM, 1) * INDEX_SCALE

    cos_m, sin_m = cm[...], sm[...]
    q = _dot(_rms(qd, qg[...]), wq_ref[...])
    scale = (QK_NOPE + QK_ROPE) ** -0.5
    for h in range(B_HEADS):
        lo = slice(h * QK_PAD, h * QK_PAD + LANES)
        hi = slice(h * QK_PAD + LANES, (h + 1) * QK_PAD)
        qb_o[:, lo] = (q[:, lo] * scale).astype(qb_o.dtype)
        qb_o[:, hi] = (_rope128(q[:, hi], cos_m, sin_m, QK_ROPE // 2, LANES) * scale).astype(qb_o.dtype)

    kv = _dot(_rms(kvd, kvg[...]), wkv_ref[...])
    krr = _rope128(kr, cos_m, sin_m, QK_ROPE // 2, LANES).astype(kb_o.dtype)
    for h in range(B_HEADS):
        kb_o[:, h * QK_PAD:h * QK_PAD + LANES] = kv[:, h * QK_NOPE:(h + 1) * QK_NOPE].astype(kb_o.dtype)
        kb_o[:, h * QK_PAD + LANES:(h + 1) * QK_PAD] = krr
    vb_o[...] = kv[:, B_HEADS * QK_NOPE:].astype(vb_o.dtype)


def _mla_prep(small, ci, si, cm, sm, qg, kvg, ikg, ikb, wq, wkv):
    n = small.shape[0]
    tm = _tile(n, 1024)
    row = lambda i: (i, 0)
    const = lambda a: pl.BlockSpec(a.shape, lambda i: (0, 0))
    tab = pl.BlockSpec((tm, LANES), row)
    wide = B_HEADS * QK_PAD
    return pl.pallas_call(
        _mla_prep_kernel,
        out_shape=[jax.ShapeDtypeStruct((n, wide), _MXU_DTYPE), jax.ShapeDtypeStruct((n, wide), _MXU_DTYPE),
                   jax.ShapeDtypeStruct((n, B_WIDTH), _MXU_DTYPE),
                   jax.ShapeDtypeStruct((n, LANES), _MXU_DTYPE), jax.ShapeDtypeStruct((n, LANES), _MXU_DTYPE),
                   jax.ShapeDtypeStruct((n, LANES), jnp.float32)],
        grid=(n // tm,),
        in_specs=[pl.BlockSpec((tm, GROUP_W), row), tab, tab, tab, tab,
                  const(qg), const(kvg), const(ikg), const(ikb), const(wq), const(wkv)],
        out_specs=[pl.BlockSpec((tm, wide), row), pl.BlockSpec((tm, wide), row), pl.BlockSpec((tm, B_WIDTH), row),
                   tab, tab, tab],
        compiler_params=_cparams(("arbitrary",)),
        name="mla_prep",
    )(small, ci, si, cm, sm, qg, kvg, ikg, ikb, wq, wkv)


def _indexer_kernel(iq_ref, iklo_ref, ikhi_ref, iw_ref, mask_o, key_scr, *, tq, tk, topk):
    i = pl.program_id(1)
    n_kb_all = mask_o.shape[1]
    n_kb = ((i + 1) * tq + tk - 1) // tk
    q_chunk = (i * tq + lax.broadcasted_iota(jnp.int32, (tq, tk), 0)) // CHUNK
    col = lax.broadcasted_iota(jnp.int32, (tq, tk), 1)
    iw = iw_ref[0]

    def score_block(kb, carry):
        klo = iklo_ref[0, pl.ds(kb * tk, tk), :]
        khi = ikhi_ref[0, pl.ds(kb * tk, tk), :]
        sc = jnp.zeros((tq, tk), jnp.float32)
        for p in range(IDX_HEADS // 2):
            qp = iq_ref[0, :, p * LANES:(p + 1) * LANES]
            for half, kk in ((0, klo), (1, khi)):
                hd = 2 * p + half
                sc = sc + jnp.maximum(_dot_nt(qp, kk), 0.0) * iw[:, hd:hd + 1]
        bits = pltpu.bitcast(sc, jnp.int32)
        key = jnp.where(bits < 0, bits ^ 0x7FFFFFFF, bits)
        visible = (kb * tk + col) // CHUNK <= q_chunk
        key_scr[kb] = jnp.where(visible, key, INT_MIN)
        return carry

    lax.fori_loop(0, n_kb, score_block, 0)

    def search_cond(state):
        it, _, settled = state
        return jnp.logical_and(it < 32, jnp.min(settled) == 0)

    def bit_pass(state):
        it, t_u, settled = state
        cand_u = t_u | lax.shift_left(jnp.int32(1), 31 - it)
        cand_s = cand_u ^ INT_MIN

        def count_block(kb, acc):
            blk = key_scr[kb]
            for c in range(tk // LANES):
                acc = acc + jnp.where(blk[:, c * LANES:(c + 1) * LANES] >= cand_s, 1.0, 0.0)
            return acc

        acc = lax.fori_loop(0, n_kb, count_block, jnp.zeros((tq, LANES), jnp.float32))
        cnt = jnp.sum(acc, axis=1, keepdims=True)
        open_row = settled == 0
        t_u = jnp.where(jnp.logical_and(open_row, cnt >= topk), cand_u, t_u)
        settled = jnp.where(jnp.logical_and(open_row, cnt == topk), 1, settled)
        return it + 1, t_u, settled

    state = (jnp.int32(0), jnp.zeros((tq, 1), jnp.int32), jnp.zeros((tq, 1), jnp.int32))
    state = lax.fori_loop(0, UNCHECKED_PASSES, lambda _, st: bit_pass(st), state)
    _, t_u, _ = lax.while_loop(search_cond, bit_pass, state)
    t_s = jnp.maximum(t_u ^ INT_MIN, INT_MIN + 1)

    for kb in range(n_kb_all):
        @pl.when(kb < n_kb)
        def _():
            mask_o[0, kb] = jnp.where(key_scr[kb] >= t_s, 0.0, NEG_BIG).astype(mask_o.dtype)

        @pl.when(kb >= n_kb)
        def _():
            mask_o[0, kb] = jnp.full((tq, tk), NEG_BIG, mask_o.dtype)


def _indexer(iq3, iklo3, ikhi3, iw3, topk):
    b, s, _ = iq3.shape
    tq = _tile(s, 128)
    tk = _tile(s, KEY_BLOCK)
    return pl.pallas_call(
        functools.partial(_indexer_kernel, tq=tq, tk=tk, topk=topk),
        out_shape=jax.ShapeDtypeStruct((b, s // tk, s, tk), jnp.bfloat16),
        grid=(b, s // tq),
        in_specs=[pl.BlockSpec((1, tq, GROUP_W), lambda bi, i: (bi, i, 0)),
                  pl.BlockSpec((1, s, LANES), lambda bi, i: (bi, 0, 0)),
                  pl.BlockSpec((1, s, LANES), lambda bi, i: (bi, 0, 0)),
                  pl.BlockSpec((1, tq, LANES), lambda bi, i: (bi, i, 0))],
        out_specs=pl.BlockSpec((1, s // tk, tq, tk), lambda bi, i: (bi, 0, i, 0)),
        scratch_shapes=[pltpu.VMEM((s // tk, tq, tk), jnp.int32)],
        compiler_params=_cparams(("arbitrary", "arbitrary")),
        name="indexer",
    )(iq3, iklo3, ikhi3, iw3)


def _softmax_heads(q_ref, k_ref, v_ref, heads, dqk, dv, n_plain, n_kb, tk, bias_fn, s_scr):
    tq = q_ref.shape[1]
    lane_tiles = [slice(c * LANES, (c + 1) * LANES) for c in range(tk // LANES)]

    def scores(masked):
        def body(kb, mxs):
            rows = pl.ds(pl.multiple_of(kb * tk, tk), tk)
            out = []
            for g, h in enumerate(heads):
                s = _dot_nt(q_ref[0, :, h * dqk:(h + 1) * dqk], k_ref[0, rows, h * dqk:(h + 1) * dqk])
                if masked:
                    s = bias_fn(kb, s)
                s_scr[g, kb] = s
                mx = mxs[g]
                for sl in lane_tiles:
                    mx = jnp.maximum(mx, s[:, sl])
                out.append(mx)
            return tuple(out)
        return body

    mxs = tuple(jnp.full((tq, LANES), NEG_BIG, jnp.float32) for _ in heads)
    mxs = lax.fori_loop(0, n_plain, scores(False), mxs)
    mxs = lax.fori_loop(n_plain, n_kb, scores(True), mxs)
    ms = [jnp.max(mx, axis=1, keepdims=True) for mx in mxs]

    def weigh(kb, carry):
        rows = pl.ds(pl.multiple_of(kb * tk, tk), tk)
        out = []
        for g, h in enumerate(heads):
            l, acc = carry[g]
            p = jnp.exp(s_scr[g, kb] - ms[g])
            for sl in lane_tiles:
                l = l + p[:, sl]
            out.append((l, acc + _dot(p, v_ref[0, rows, h * dv:(h + 1) * dv])))
        return tuple(out)

    zero = (jnp.zeros((tq, LANES), jnp.float32), jnp.zeros((tq, dv), jnp.float32))
    res = lax.fori_loop(0, n_kb, weigh, tuple(zero for _ in heads))
    return [acc / jnp.sum(l, axis=1, keepdims=True) for l, acc in res]


def _attn_a_kernel(q_ref, k_ref, v_ref, bias_ref, g_ref, o_ref, o_scr, s_scr, *, tq, tk):
    i = pl.program_id(1)
    n_kb = ((i + 1) * tq + tk - 1) // tk

    def bias_fn(kb, s):
        return s + bias_ref[0, kb].astype(jnp.float32)

    for h0 in range(0, A_HEADS, HEAD_GROUP):
        heads = list(range(h0, h0 + HEAD_GROUP))
        outs = _softmax_heads(q_ref, k_ref, v_ref, heads, A_HEAD_DIM, A_HEAD_DIM, 0, n_kb, tk, bias_fn, s_scr)
        for h, o in zip(heads, outs):
            o_scr[:, h * A_HEAD_DIM:(h + 1) * A_HEAD_DIM] = o
    o_ref[0] = _rms(o_scr[...], g_ref[...]).astype(o_ref.dtype)


def _attn_a(q3, k3, v3, mask4, g):
    b, s, w = q3.shape
    tk = mask4.shape[3]
    tq = _tile(s, 256)
    full = lambda bi, i: (bi, 0, 0)
    return pl.pallas_call(
        functools.partial(_attn_a_kernel, tq=tq, tk=tk),
        out_shape=jax.ShapeDtypeStruct((b, s, w), _MXU_DTYPE),
        grid=(b, s // tq),
        in_specs=[pl.BlockSpec((1, tq, w), lambda bi, i: (bi, i, 0)),
                  pl.BlockSpec((1, s, w), full, pipeline_mode=pl.Buffered(1)),
                  pl.BlockSpec((1, s, w), full, pipeline_mode=pl.Buffered(1)),
                  pl.BlockSpec((1, s // tk, tq, tk), lambda bi, i: (bi, 0, i, 0)),
                  pl.BlockSpec((1, w), lambda bi, i: (0, 0))],
        out_specs=pl.BlockSpec((1, tq, w), lambda bi, i: (bi, i, 0)),
        scratch_shapes=[pltpu.VMEM((tq, w), jnp.float32), pltpu.VMEM((HEAD_GROUP, s // tk, tq, tk), jnp.float32)],
        compiler_params=_cparams(("arbitrary", "arbitrary")),
        name="attn_a",
    )(q3, k3, v3, mask4, g)


def _attn_b_kernel(q_ref, k_ref, v_ref, g_ref, o_ref, o_scr, s_scr, *, tq, tk):
    i = pl.program_id(1)
    n_kb = ((i + 1) * tq + tk - 1) // tk
    n_plain = (i * tq) // tk
    q_chunk = (i * tq + lax.broadcasted_iota(jnp.int32, (tq, tk), 0)) // CHUNK
    col = lax.broadcasted_iota(jnp.int32, (tq, tk), 1)

    def bias_fn(kb, s):
        return jnp.where((kb * tk + col) // CHUNK <= q_chunk, s, NEG_BIG)

    for h0 in range(0, B_HEADS, HEAD_GROUP):
        heads = list(range(h0, h0 + HEAD_GROUP))
        outs = _softmax_heads(q_ref, k_ref, v_ref, heads, QK_PAD, V_HEAD, n_plain, n_kb, tk, bias_fn, s_scr)
        for h, o in zip(heads, outs):
            o_scr[:, h * V_HEAD:(h + 1) * V_HEAD] = o
    o_ref[0] = _rms(o_scr[...], g_ref[...]).astype(o_ref.dtype)


def _attn_b(q3, k3, v3, g):
    b, s, wq = q3.shape
    wv = v3.shape[2]
    tq = _tile(s, 256)
    tk = _tile(s, KEY_BLOCK)
    full = lambda bi, i: (bi, 0, 0)
    return pl.pallas_call(
        functools.partial(_attn_b_kernel, tq=tq, tk=tk),
        out_shape=jax.ShapeDtypeStruct((b, s, wv), _MXU_DTYPE),
        grid=(b, s // tq),
        in_specs=[pl.BlockSpec((1, tq, wq), lambda bi, i: (bi, i, 0)),
                  pl.BlockSpec((1, s, wq), full, pipeline_mode=pl.Buffered(1)),
                  pl.BlockSpec((1, s, wv), full, pipeline_mode=pl.Buffered(1)),
                  pl.BlockSpec((1, wv), lambda bi, i: (0, 0))],
        out_specs=pl.BlockSpec((1, tq, wv), lambda bi, i: (bi, i, 0)),
        scratch_shapes=[pltpu.VMEM((tq, wv), jnp.float32), pltpu.VMEM((HEAD_GROUP, s // tk, tq, tk), jnp.float32)],
        compiler_params=_cparams(("arbitrary", "arbitrary")),
        name="attn_b",
    )(q3, k3, v3, g)


def _outproj_kernel(ma_ref, mb_ref, w_ref, x_ref, g1_ref, sc2_ref, sh2_ref, lg_ref, lb_ref, wrh_ref, wrl_ref, br_ref,
                    x1_o, h2_o, lgt_o, *, alpha, n_experts, n_split):
    rows_per = x_ref.shape[0] // n_split
    lane = lax.broadcasted_iota(jnp.int32, (rows_per, LANES), 1)
    for part in range(n_split):
        r = slice(part * rows_per, (part + 1) * rows_per)
        mix = (jnp.dot(ma_ref[r, :], w_ref[:A_WIDTH, :], preferred_element_type=jnp.float32)
               + jnp.dot(mb_ref[r, :], w_ref[A_WIDTH:, :], preferred_element_type=jnp.float32))
        x1 = _ln_plain(alpha * x_ref[r, :] + g1_ref[0] * mix) * lg_ref[...] + lb_ref[...]
        x1_o[r, :] = x1
        h2 = _ln_plain(x1) * (1.0 + sc2_ref[0]) + sh2_ref[0]
        h2_o[r, :] = h2
        h_hi = h2.astype(_MXU_DTYPE)
        h_lo = (h2 - h_hi.astype(jnp.float32)).astype(_MXU_DTYPE)
        logits = (jnp.dot(h_hi, wrh_ref[...], preferred_element_type=jnp.float32)
                  + jnp.dot(h_lo, wrh_ref[...], preferred_element_type=jnp.float32)
                  + jnp.dot(h_hi, wrl_ref[...], preferred_element_type=jnp.float32)) + br_ref[...]
        lgt_o[r, :] = jnp.where(lane < n_experts, logits, -jnp.inf)


def _outproj(ma, mb, w_out, x2, ada3, lg, lb, wr, br, seq, alpha, n_experts):
    n, d = x2.shape
    tm = _tile(seq, 512)
    n_split = 2 if tm % 32 == 0 else 1
    wr_hi = wr.astype(_MXU_DTYPE)
    wr_lo = (wr - wr_hi.astype(jnp.float32)).astype(_MXU_DTYPE)
    row = lambda i: (i, 0)
    const = lambda a: pl.BlockSpec(a.shape, lambda i: (0,) * a.ndim)
    ada_spec = lambda col: pl.BlockSpec((1, 1, d), lambda i, col=col: (i * tm // seq, 0, col))
    return pl.pallas_call(
        functools.partial(_outproj_kernel, alpha=alpha, n_experts=n_experts, n_split=n_split),
        out_shape=[jax.ShapeDtypeStruct((n, d), jnp.float32),
                   jax.ShapeDtypeStruct((n, d), jnp.float32),
                   jax.ShapeDtypeStruct((n, LANES), jnp.float32)],
        grid=(n // tm,),
        in_specs=[pl.BlockSpec((tm, A_WIDTH), row), pl.BlockSpec((tm, B_WIDTH), row),
                  pl.BlockSpec(w_out.shape, lambda i: (0, 0), pipeline_mode=pl.Buffered(1)),
                  pl.BlockSpec((tm, d), row), ada_spec(2), ada_spec(4), ada_spec(3),
                  const(lg), const(lb), const(wr_hi), const(wr_lo), const(br)],
        out_specs=[pl.BlockSpec((tm, d), row), pl.BlockSpec((tm, d), row), pl.BlockSpec((tm, LANES), row)],
        compiler_params=_cparams(("arbitrary",)),
        name="outproj",
    )(ma, mb, w_out, x2, ada3, ada3, ada3, lg, lb, wr_hi, wr_lo, br)


def _route_kernel(lgt_ref, top_o, gate_o, rank_o, cnt_o, carry):
    tb = lgt_ref.shape[0]

    @pl.when(pl.program_id(0) == 0)
    def _():
        carry[...] = jnp.zeros_like(carry)

    lane = lax.broadcasted_iota(jnp.int32, (tb, LANES), 1)
    lane_f = lane.astype(jnp.float32)
    work = lgt_ref[...]
    vals, idxs, hots = [], [], []
    for _ in range(TOP_K_EXPERTS):
        m = jnp.max(work, axis=1, keepdims=True)
        idx = jnp.min(jnp.where(work == m, lane_f, float(LANES)), axis=1, keepdims=True)
        hot = lane_f == idx
        vals.append(m)
        idxs.append(idx)
        hots.append(hot)
        work = jnp.where(hot, -jnp.inf, work)

    exps = [jnp.exp(v - vals[0]) for v in vals]
    denom = exps[0] + exps[1] + exps[2] + exps[3]

    member = jnp.zeros((tb, LANES), jnp.float32)
    for hot in hots:
        member = member + jnp.where(hot, 1.0, 0.0)
    r = lax.broadcasted_iota(jnp.int32, (tb, tb), 0)
    c = lax.broadcasted_iota(jnp.int32, (tb, tb), 1)
    before = jnp.where(c < r, 1.0, 0.0).astype(jnp.bfloat16)
    prefix = jnp.dot(before, member.astype(jnp.bfloat16), preferred_element_type=jnp.float32) + carry[...]

    top = jnp.zeros((tb, LANES), jnp.int32)
    gate = jnp.zeros((tb, LANES), jnp.float32)
    rank = jnp.zeros((tb, LANES), jnp.int32)
    for k in range(TOP_K_EXPERTS):
        rk = jnp.sum(jnp.where(hots[k], prefix, 0.0), axis=1, keepdims=True)
        top = jnp.where(lane == k, idxs[k].astype(jnp.int32), top)
        gate = jnp.where(lane == k, exps[k] / denom, gate)
        rank = jnp.where(lane == k, rk.astype(jnp.int32), rank)
    top_o[...] = top
    gate_o[...] = gate
    rank_o[...] = rank
    carry[...] = carry[...] + jnp.sum(member, axis=0, keepdims=True)
    cnt_o[...] = jnp.broadcast_to(carry[...], cnt_o.shape).astype(jnp.int32)


def _route(logits):
    n = logits.shape[0]
    tb = _tile(n, 512)
    row = pl.BlockSpec((tb, LANES), lambda i: (i, 0))
    return pl.pallas_call(
        _route_kernel,
        out_shape=[jax.ShapeDtypeStruct((n, LANES), jnp.int32), jax.ShapeDtypeStruct((n, LANES), jnp.float32),
                   jax.ShapeDtypeStruct((n, LANES), jnp.int32), jax.ShapeDtypeStruct((8, LANES), jnp.int32)],
        grid=(n // tb,),
        in_specs=[row],
        out_specs=[row, row, row, pl.BlockSpec((8, LANES), lambda i: (0, 0))],
        scratch_shapes=[pltpu.VMEM((1, LANES), jnp.float32)],
        compiler_params=_cparams(("arbitrary",)),
        name="route",
    )(logits)


def _row_copy(src, s_row, dst, d_row, sem):
    return pltpu.make_async_copy(src.at[pl.ds(s_row, 1), :], dst.at[pl.ds(d_row, 1), :], sem)


def _dispatch_kernel(dest_ref, fill_ref, h_ref, xs_hbm, zeros, sem, zsem, *, td, sub, n_experts):
    base = pl.program_id(0) * td

    @pl.when(pl.program_id(0) == 0)
    def _():
        zeros[...] = jnp.zeros_like(zeros)
        tail0 = fill_ref[2 * n_experts]
        n_tail = fill_ref[2 * n_experts + 1]

        def tail_copy(j):
            rows = pl.ds(pl.multiple_of(tail0 + j * sub, sub), sub)
            return pltpu.make_async_copy(zeros, xs_hbm.at[rows, :], zsem)

        def for_pad_rows(fn):
            def per_expert(e, c):
                first = fill_ref[e]
                lax.fori_loop(0, fill_ref[n_experts + e], lambda r, c2: (fn(first + r), c2)[1], 0)
                return c
            lax.fori_loop(0, n_experts, per_expert, 0)

        for_pad_rows(lambda r: _row_copy(zeros, 0, xs_hbm, r, zsem).start())
        lax.fori_loop(0, n_tail, lambda j, c: (tail_copy(j).start(), c)[1], 0)
        for_pad_rows(lambda r: _row_copy(zeros, 0, xs_hbm, r, zsem).wait())
        lax.fori_loop(0, n_tail, lambda j, c: (tail_copy(j).wait(), c)[1], 0)

    def issue(t, c):
        for k in range(TOP_K_EXPERTS):
            _row_copy(h_ref, t, xs_hbm, dest_ref[(base + t) * TOP_K_EXPERTS + k], sem).start()
        return c

    lax.fori_loop(0, td, issue, 0, unroll=ROW_DMA_UNROLL)
    landed = xs_hbm.at[pl.ds(0, td * TOP_K_EXPERTS), :]
    pltpu.make_async_copy(landed, landed, sem).wait()


def _dispatch(dest_flat, fill, h2, rows, sub, n_experts):
    n, d = h2.shape
    td = _tile(n, 512)
    return pl.pallas_call(
        functools.partial(_dispatch_kernel, td=td, sub=sub, n_experts=n_experts),
        out_shape=jax.ShapeDtypeStruct((rows, d), h2.dtype),
        grid_spec=pltpu.PrefetchScalarGridSpec(
            num_scalar_prefetch=2, grid=(n // td,),
            in_specs=[pl.BlockSpec((td, d), lambda i, dr, fl: (i, 0))],
            out_specs=pl.BlockSpec(memory_space=pl.ANY),
            scratch_shapes=[pltpu.VMEM((sub, d), h2.dtype), pltpu.SemaphoreType.DMA(()),
                            pltpu.SemaphoreType.DMA(())]),
        compiler_params=pltpu.CompilerParams(dimension_semantics=("arbitrary",), has_side_effects=True,
                                             disable_bounds_checks=True),
        name="dispatch",
    )(dest_flat, fill, h2)


def _swiglu_compact(gu):
    tf2 = gu.shape[1]
    nxt = pltpu.roll(gu, tf2 - 1, 1)
    g = jnp.minimum(gu, SWIGLU_LIMIT)
    u = jnp.clip(nxt, -SWIGLU_LIMIT, SWIGLU_LIMIT)
    act = ((u + 1.0) * (g * jax.nn.sigmoid(SWIGLU_ALPHA * g))).astype(_MXU_DTYPE)
    r = lax.broadcasted_iota(jnp.int32, (2 * LANES, LANES), 0)
    c = lax.broadcasted_iota(jnp.int32, (2 * LANES, LANES), 1)
    pick = jnp.where(r == 2 * c, 1.0, 0.0).astype(_MXU_DTYPE)
    parts = [jnp.dot(act[:, t * 2 * LANES:(t + 1) * 2 * LANES], pick, preferred_element_type=jnp.float32)
             for t in range(tf2 // (2 * LANES))]
    return jnp.concatenate(parts, axis=1).astype(_MXU_DTYPE)


def _moe_gemm_kernel(sbe_ref, row0_ref, nsub_ref, xs_hbm, wgu_ref, bgu_ref, wd_ref, bd_ref, y_hbm,
                     xstage, xb, act, wb, wdb, ystage, gu_scr, sem_x, sem_y, *, sub, n_f):
    del sbe_ref
    sb = pl.program_id(0)
    t = pl.program_id(1)
    nsub = nsub_ref[sb]
    row0 = row0_ref[sb]
    tf = wd_ref.shape[1]

    def x_copy(c, slot):
        rows = pl.ds(pl.multiple_of(row0 + c * sub, sub), sub)
        return pltpu.make_async_copy(xs_hbm.at[rows, :], xstage.at[slot], sem_x.at[slot])

    def y_copy(s, slot):
        rows = pl.ds(pl.multiple_of(row0 + s * sub, sub), sub)
        return pltpu.make_async_copy(ystage.at[slot], y_hbm.at[rows, :], sem_y.at[slot])

    @pl.when(jnp.logical_and(nsub > 0, t < n_f))
    def _():
        wb[...] = wgu_ref[0].astype(wb.dtype)
        wdb[pl.ds(pl.multiple_of(t * tf, tf), tf), :] = wd_ref[0].astype(wdb.dtype)
        bias = bgu_ref[0]

        def gate_up(x):
            return jnp.dot(x, wb[...], preferred_element_type=jnp.float32) + bias

        @pl.when(t == 0)
        def _():
            x_copy(0, 0).start()

            def first(c, carry):
                slot = c % 2

                @pl.when(c + 1 < nsub)
                def _():
                    x_copy(c + 1, 1 - slot).start()

                x_copy(c, slot).wait()
                rows = pl.ds(pl.multiple_of(c * sub, sub), sub)
                xv = xstage[slot].astype(xb.dtype)
                xb[rows, :] = xv
                act[0, rows, :] = _swiglu_compact(gate_up(xv))
                return carry

            lax.fori_loop(0, nsub, first, 0)

        @pl.when(t > 0)
        def _():
            n_pairs = nsub // 2
            pair_rows = lambda p: pl.ds(pl.multiple_of(p * 2 * sub, 2 * sub), 2 * sub)

            @pl.when(n_pairs > 0)
            def _():
                gu_scr[...] = gate_up(xb[pair_rows(0), :])

                def up(p, carry):
                    prev = gu_scr[...]
                    gu_scr[...] = gate_up(xb[pair_rows(p), :])
                    act[t, pair_rows(p - 1), :] = _swiglu_compact(prev)
                    return carry

                lax.fori_loop(1, n_pairs, up, 0)
                act[t, pair_rows(n_pairs - 1), :] = _swiglu_compact(gu_scr[...])

            @pl.when(nsub % 2 == 1)
            def _():
                rows = pl.ds(pl.multiple_of((nsub - 1) * sub, sub), sub)
                act[t, rows, :] = _swiglu_compact(gate_up(xb[rows, :]))

    @pl.when(jnp.logical_and(nsub > 0, t == n_f))
    def _():
        bias = bd_ref[0]

        def down(s, carry):
            slot = s % 2
            rows = pl.ds(pl.multiple_of(s * sub, sub), sub)
            a = jnp.concatenate([act[f, rows, :] for f in range(n_f)], axis=1)
            yt = jnp.dot(a, wdb[...], preferred_element_type=jnp.float32) + bias

            @pl.when(s >= 2)
            def _():
                y_copy(s - 2, slot).wait()

            ystage[slot] = yt
            y_copy(s, slot).start()
            return carry

        lax.fori_loop(0, nsub, down, 0)

        @pl.when(nsub >= 2)
        def _():
            y_copy(nsub - 2, nsub % 2).wait()

        y_copy(nsub - 1, (nsub - 1) % 2).wait()


def _moe_gemm(sb_e, sb_row0, sb_nsub, xs, wgu, bgu3, wd, bd3, sub, max_sub):
    e, d, ff2 = wgu.shape
    ff = ff2 // 2
    tf = _tile(ff, 256)
    n_f = ff // tf
    n_sb = sb_e.shape[0]
    rm = max_sub * sub
    f_idx = lambda sb, t, ns: jnp.where(ns[sb] > 0, jnp.minimum(t, n_f - 1), n_f - 1)
    return pl.pallas_call(
        functools.partial(_moe_gemm_kernel, sub=sub, n_f=n_f),
        out_shape=jax.ShapeDtypeStruct(xs.shape, jnp.float32),
        grid_spec=pltpu.PrefetchScalarGridSpec(
            num_scalar_prefetch=3, grid=(n_sb, n_f + 1),
            in_specs=[pl.BlockSpec(memory_space=pl.ANY),
                      pl.BlockSpec((1, d, 2 * tf), lambda sb, t, se, r0, ns: (se[sb], 0, f_idx(sb, t, ns))),
                      pl.BlockSpec((1, 1, 2 * tf), lambda sb, t, se, r0, ns: (se[sb], 0, f_idx(sb, t, ns))),
                      pl.BlockSpec((1, tf, d), lambda sb, t, se, r0, ns: (se[sb], f_idx(sb, t, ns), 0)),
                      pl.BlockSpec((1, 1, d), lambda sb, t, se, r0, ns: (se[sb], 0, 0))],
            out_specs=pl.BlockSpec(memory_space=pl.ANY),
            scratch_shapes=[pltpu.VMEM((2, sub, d), jnp.float32),
                            pltpu.VMEM((rm, d), _MXU_DTYPE),
                            pltpu.VMEM((n_f, rm, tf), _MXU_DTYPE),
                            pltpu.VMEM((d, 2 * tf), _MXU_DTYPE),
                            pltpu.VMEM((ff, d), _MXU_DTYPE),
                            pltpu.VMEM((2, sub, d), jnp.float32),
                            pltpu.VMEM((2 * sub, 2 * tf), jnp.float32),
                            pltpu.SemaphoreType.DMA((2,)), pltpu.SemaphoreType.DMA((2,))]),
        input_output_aliases={3: 0},
        compiler_params=_cparams(("arbitrary", "arbitrary")),
        name="moe_gemm",
    )(sb_e, sb_row0, sb_nsub, xs, wgu, bgu3, wd, bd3)


def _combine_kernel(dest_ref, y_hbm, gate_ref, x1_ref, g2_ref, lg_ref, lb_ref, o_ref, buf, sem, *, tc, alpha):
    base = pl.program_id(0) * tc

    def issue(t, c):
        for k in range(TOP_K_EXPERTS):
            _row_copy(y_hbm, dest_ref[(base + t) * TOP_K_EXPERTS + k], buf, k * tc + t, sem).start()
        return c

    lax.fori_loop(0, tc, issue, 0, unroll=ROW_DMA_UNROLL)
    pltpu.make_async_copy(y_hbm.at[pl.ds(0, tc * TOP_K_EXPERTS), :], buf, sem).wait()

    gate = gate_ref[...]
    ffn = buf[0:tc, :] * gate[:, 0:1]
    for k in range(1, TOP_K_EXPERTS):
        ffn = ffn + buf[k * tc:(k + 1) * tc, :] * gate[:, k:k + 1]
    o_ref[...] = _ln_plain(alpha * x1_ref[...] + g2_ref[0] * ffn) * lg_ref[...] + lb_ref[...]


def _combine(dest_flat, y, gates, x1, ada3, lg, lb, seq, alpha):
    n, d = x1.shape
    tc = _tile(seq, 512)
    row = lambda i, dr: (i, 0)
    return pl.pallas_call(
        functools.partial(_combine_kernel, tc=tc, alpha=alpha),
        out_shape=jax.ShapeDtypeStruct((n, d), jnp.float32),
        grid_spec=pltpu.PrefetchScalarGridSpec(
            num_scalar_prefetch=1, grid=(n // tc,),
            in_specs=[pl.BlockSpec(memory_space=pl.ANY),
                      pl.BlockSpec((tc, LANES), row), pl.BlockSpec((tc, d), row),
                      pl.BlockSpec((1, 1, d), lambda i, dr: (i * tc // seq, 0, 5)),
                      pl.BlockSpec((1, d), lambda i, dr: (0, 0)), pl.BlockSpec((1, d), lambda i, dr: (0, 0))],
            out_specs=pl.BlockSpec((tc, d), row),
            scratch_shapes=[pltpu.VMEM((TOP_K_EXPERTS * tc, d), jnp.float32),
                            pltpu.SemaphoreType.DMA(())]),
        compiler_params=pltpu.CompilerParams(dimension_semantics=("arbitrary",), vmem_limit_bytes=VMEM_LIMIT,
                                             disable_bounds_checks=True),
        name="combine",
    )(dest_flat, y, gates, x1, ada3, lg, lb)


def _pad_cols(a, width):
    return jnp.pad(a, ((0, 0), (0, width - a.shape[1])))


def _perm_w_in(w_in):
    o = np.cumsum([0, A_WIDTH, A_WIDTH, A_WIDTH, IDX_HEADS * IDX_DIM, IDX_DIM, IDX_HEADS, Q_LORA, KV_LORA, QK_ROPE])
    seg = lambda i: w_in[:, o[i]:o[i + 1]]
    ikw = _pad_cols(jnp.concatenate([seg(4), seg(5)], axis=1), LANES)
    kr = _pad_cols(seg(8), LANES)
    return jnp.concatenate([seg(0), seg(1), seg(2), seg(3), seg(6), seg(7), ikw, kr], axis=1)


def _perm_w_q_up(w):
    w = w.reshape(Q_LORA, B_HEADS, QK_NOPE + QK_ROPE)
    w = jnp.pad(w, ((0, 0), (0, 0), (0, QK_PAD - QK_NOPE - QK_ROPE)))
    return w.reshape(Q_LORA, B_HEADS * QK_PAD)


def _perm_w_kv_up(w):
    w = w.reshape(KV_LORA, B_HEADS, QK_NOPE + V_HEAD)
    return jnp.concatenate([w[:, :, :QK_NOPE].reshape(KV_LORA, -1), w[:, :, QK_NOPE:].reshape(KV_LORA, -1)], axis=1)


MOE_MAX_SUB = 9


def _moe_sub_rows(n_tokens):
    return 256 if n_tokens >= 4096 else 64


def _super_blocks(cnt, sub, n_assign):
    n_experts = cnt.shape[0]
    q = (cnt + sub - 1) // sub
    pad_ends = jnp.cumsum(q * sub)
    pad_starts = pad_ends - q * sub
    nsb_e = (q + MOE_MAX_SUB - 1) // MOE_MAX_SUB
    sb_end = jnp.cumsum(nsb_e)
    sb_start = sb_end - nsb_e
    n_sb = (n_assign // sub + n_experts + MOE_MAX_SUB - 1) // MOE_MAX_SUB + n_experts
    idx = jnp.arange(n_sb, dtype=jnp.int32)
    valid = idx < sb_end[-1]
    e_of = jnp.minimum(jnp.sum((sb_end[None, :] <= idx[:, None]).astype(jnp.int32), axis=1), n_experts - 1)
    j = idx - sb_start[e_of]
    parts = jnp.maximum(nsb_e[e_of], 1)
    base, rem = q[e_of] // parts, q[e_of] % parts
    nsub = jnp.where(valid, base + (j < rem).astype(jnp.int32), 0)
    row0 = jnp.where(valid, pad_starts[e_of] + (j * base + jnp.minimum(j, rem)) * sub, 0)
    e_last = jnp.max(jnp.where(valid, e_of, 0))
    sb_e = jnp.where(valid, e_of, e_last)
    i32 = lambda a: a.astype(jnp.int32)
    rows = (n_assign // sub + n_experts) * sub
    fill = jnp.concatenate([pad_starts + cnt, q * sub - cnt, jnp.stack([pad_ends[-1], (rows - pad_ends[-1]) // sub])])
    return pad_starts, i32(sb_e), i32(row0), i32(nsub), i32(fill)


def kernel(x, c, positions, w_ada, b_ada, w_in, idx_k_norm_g, idx_k_norm_b, q_norm_g, w_q_up, kv_norm_g, w_kv_up,
           out_norm_a_g, out_norm_b_g, w_out, ln_mix_g, ln_mix_b, w_router, b_router, w_gate_up, b_gate_up,
           w_down, b_down, ln_ffn_g, ln_ffn_b):
    bsz, seq, d = x.shape
    depth = w_ada.shape[0]
    n_experts = w_router.shape[2]
    n = bsz * seq
    alpha = (2 * depth) ** 0.25
    topk = min(TOPK_MAX, seq // 4)
    sub = _moe_sub_rows(n)
    n_assign = n * TOP_K_EXPERTS
    rows = (n_assign // sub + n_experts) * sub

    ca, sa, ci, si, cm, sm = _rope_tables(positions.reshape(n, 1))
    c8 = jnp.pad(c, ((0, 8 - bsz), (0, 0)))
    x2 = x.reshape(n, d)
    row = lambda a: a.reshape(1, -1)

    for l in range(depth):
        ada3 = _ada(c8, w_ada[l], row(b_ada[l]))[:bsz].reshape(bsz, 1, 6 * d)

        aq, ak, av, iq, small = _proj(x2, ada3, _perm_w_in(w_in[l]).astype(_MXU_DTYPE), ca, sa, ci, si, seq)
        qb, kb, vb, iklo, ikhi, iw = _mla_prep(
            small, ci, si, cm, sm, row(q_norm_g[l]), row(kv_norm_g[l]),
            _pad_cols(row(idx_k_norm_g[l]), LANES), _pad_cols(row(idx_k_norm_b[l]), LANES),
            _perm_w_q_up(w_q_up[l]).astype(_MXU_DTYPE), _perm_w_kv_up(w_kv_up[l]).astype(_MXU_DTYPE))
        b3 = lambda a: a.reshape(bsz, seq, a.shape[-1])
        mask = _indexer(b3(iq), b3(iklo), b3(ikhi), b3(iw), topk)
        out_a = _attn_a(b3(aq), b3(ak), b3(av), mask, row(out_norm_a_g[l]))
        out_b = _attn_b(b3(qb), b3(kb), b3(vb), row(out_norm_b_g[l]))

        x1, h2, logits = _outproj(
            out_a.reshape(n, A_WIDTH), out_b.reshape(n, B_WIDTH), w_out[l].astype(_MXU_DTYPE), x2, ada3,
            row(ln_mix_g[l]), row(ln_mix_b[l]), _pad_cols(w_router[l], LANES), _pad_cols(row(b_router[l]), LANES),
            seq, alpha, n_experts)
        top, gates, rank, counts = _route(logits)

        pad_starts, sb_e, sb_row0, sb_nsub, fill = _super_blocks(counts[0, :n_experts], sub, n_assign)
        dest = (pad_starts[top[:, :TOP_K_EXPERTS]] + rank[:, :TOP_K_EXPERTS]).reshape(-1).astype(jnp.int32)

        xs = _dispatch(dest, fill, h2, rows, sub, n_experts)
        y = _moe_gemm(sb_e, sb_row0, sb_nsub, xs, w_gate_up[l], b_gate_up[l].reshape(n_experts, 1, -1), w_down[l],
                      b_down[l].reshape(n_experts, 1, -1), sub, MOE_MAX_SUB)
        x2 = _combine(dest, y, gates, x1, ada3, row(ln_ffn_g[l]), row(ln_ffn_b[l]), seq, alpha)

    return x2.reshape(bsz, seq, d)
```

```python
import functools
import math

import jax
import jax.numpy as jnp
import numpy as np
from jax import lax
from jax.experimental import pallas as pl
from jax.experimental.pallas import tpu as pltpu

CHUNK = 64
ROPE_THETA = 500000.0
EPS = 1e-5
A_HEADS = 8
A_HEAD_DIM = 128
A_ROT_DIM = A_HEAD_DIM // 4
IDX_HEADS = 16
IDX_DIM = 64
IDX_ROT_DIM = IDX_DIM // 4
TOPK_MAX = 256
INDEX_SCALE = (IDX_DIM ** -0.5) * (IDX_HEADS ** -0.5)
B_HEADS = 8
Q_LORA = 512
KV_LORA = 256
QK_NOPE = 128
QK_ROPE = 64
V_HEAD = 128
A_WIDTH = A_HEADS * A_HEAD_DIM
B_WIDTH = B_HEADS * V_HEAD
TOP_K_EXPERTS = 4
SWIGLU_LIMIT = 7.0
SWIGLU_ALPHA = 1.702

LANES = 128
QK_PAD = 256
GROUP_W = 1024
VMEM_LIMIT = 56 * 1024 * 1024
NEG_BIG = -1e30
UNCHECKED_PASSES = 22
KEY_BLOCK = 1024
HEAD_GROUP = 4
ROW_DMA_UNROLL = 8
INT_MIN = -2 ** 31

_MXU_DTYPE = jnp.bfloat16


def _cparams(sem, vmem=VMEM_LIMIT):
    return pltpu.CompilerParams(dimension_semantics=sem, vmem_limit_bytes=vmem)


def _tile(n, t):
    t = min(n, t)
    assert n % t == 0, (n, t)
    return t


def _dot(a, b):
    return jnp.dot(a.astype(_MXU_DTYPE), b.astype(_MXU_DTYPE), preferred_element_type=jnp.float32)


def _dot_nt(a, b):
    return lax.dot_general(a.astype(_MXU_DTYPE), b.astype(_MXU_DTYPE), (((1,), (1,)), ((), ())),
                           preferred_element_type=jnp.float32)


def _ln_plain(x):
    mu = jnp.mean(x, axis=-1, keepdims=True)
    d = x - mu
    var = jnp.mean(d * d, axis=-1, keepdims=True)
    return d * lax.rsqrt(var + EPS)


def _rms(x, g):
    return x * lax.rsqrt(jnp.mean(x * x, axis=-1, keepdims=True) + EPS) * g


def _rope128(x, cos, sin, half, group):
    lane = lax.broadcasted_iota(jnp.int32, x.shape, 1) % group
    partner = jnp.where(lane < half, pltpu.roll(x, LANES - half, 1), pltpu.roll(x, half, 1))
    return x * cos + partner * sin


def _ada_kernel(c_ref, w_ref, b_ref, o_ref):
    c = c_ref[...]
    o_ref[...] = _dot(c * jax.nn.sigmoid(c), w_ref[...]) + b_ref[...]


def _ada(c8, w, b):
    d, n = w.shape
    tn = math.gcd(n, 1024)
    return pl.pallas_call(
        _ada_kernel,
        out_shape=jax.ShapeDtypeStruct((8, n), jnp.float32),
        grid=(n // tn,),
        in_specs=[pl.BlockSpec((8, d), lambda j: (0, 0)),
                  pl.BlockSpec((d, tn), lambda j: (0, j)),
                  pl.BlockSpec((1, tn), lambda j: (0, j))],
        out_specs=pl.BlockSpec((8, tn), lambda j: (0, j)),
        compiler_params=_cparams(("arbitrary",)),
        name="ada",
    )(c8, w, b)


def _rope_rows(rot, group):
    half = rot // 2
    inv_freq = ROPE_THETA ** (-jnp.arange(half, dtype=jnp.float32) / half)
    lane = np.arange(LANES) % group
    freq = jnp.where(lane < rot, inv_freq[lane % half], 0.0)
    sign = np.where(lane < half, -1.0, np.where(lane < rot, 1.0, 0.0)).astype(np.float32)
    return freq.reshape(1, LANES).astype(jnp.float32), jnp.asarray(sign).reshape(1, LANES)


def _rope_tables_kernel(pos_ref, fa, sa, fi, si, fm, sm, ca_o, sa_o, ci_o, si_o, cm_o, sm_o):
    pos = pos_ref[...].astype(jnp.float32)
    for f, s, c_o, s_o in ((fa, sa, ca_o, sa_o), (fi, si, ci_o, si_o), (fm, sm, cm_o, sm_o)):
        ang = pos * f[...]
        c_o[...] = jnp.cos(ang)
        s_o[...] = jnp.sin(ang) * s[...]


def _rope_tables(pos_col):
    n = pos_col.shape[0]
    t = _tile(n, 1024)
    rows = (*_rope_rows(A_ROT_DIM, A_HEAD_DIM), *_rope_rows(IDX_ROT_DIM, IDX_DIM), *_rope_rows(QK_ROPE, LANES))
    row_spec = pl.BlockSpec((1, LANES), lambda i: (0, 0))
    tab_spec = pl.BlockSpec((t, LANES), lambda i: (i, 0))
    return pl.pallas_call(
        _rope_tables_kernel,
        out_shape=[jax.ShapeDtypeStruct((n, LANES), jnp.float32)] * 6,
        grid=(n // t,),
        in_specs=[pl.BlockSpec((t, 1), lambda i: (i, 0))] + [row_spec] * 6,
        out_specs=[tab_spec] * 6,
        compiler_params=_cparams(("arbitrary",)),
        name="rope_tables",
    )(pos_col, *rows)


def _proj_kernel(x_ref, sc_ref, sh_ref, w_ref, ca, sa, ci, si, aq_o, ak_o, av_o, iq_o, sm_o):
    h = (_ln_plain(x_ref[...]) * (1.0 + sc_ref[0]) + sh_ref[0]).astype(w_ref.dtype)

    def group(j):
        return jnp.dot(h, w_ref[:, j * GROUP_W:(j + 1) * GROUP_W], preferred_element_type=jnp.float32)

    def rope_heads(acc, o_ref, cos_ref, sin_ref, half, group_w, scale):
        cos, sin = cos_ref[...], sin_ref[...]
        for t in range(GROUP_W // LANES):
            sl = slice(t * LANES, (t + 1) * LANES)
            o_ref[:, sl] = (_rope128(acc[:, sl], cos, sin, half, group_w) * scale).astype(o_ref.dtype)

    rope_heads(group(0), aq_o, ca, sa, A_ROT_DIM // 2, A_HEAD_DIM, A_HEAD_DIM ** -0.5)
    rope_heads(group(1), ak_o, ca, sa, A_ROT_DIM // 2, A_HEAD_DIM, 1.0)
    av_o[...] = group(2).astype(av_o.dtype)
    rope_heads(group(3), iq_o, ci, si, IDX_ROT_DIM // 2, IDX_DIM, 1.0)
    sm_o[...] = group(4)


def _proj(x2, ada3, w_perm, ca, sa, ci, si, seq):
    n, d = x2.shape
    tm = _tile(seq, 512)
    row = lambda i: (i, 0)
    ada_spec = lambda col: pl.BlockSpec((1, 1, d), lambda i, col=col: (i * tm // seq, 0, col))
    big = lambda dt: jax.ShapeDtypeStruct((n, GROUP_W), dt)
    return pl.pallas_call(
        _proj_kernel,
        out_shape=[big(_MXU_DTYPE)] * 4 + [big(jnp.float32)],
        grid=(n // tm,),
        in_specs=[pl.BlockSpec((tm, d), row), ada_spec(1), ada_spec(0),
                  pl.BlockSpec(w_perm.shape, lambda i: (0, 0), pipeline_mode=pl.Buffered(1))]
                 + [pl.BlockSpec((tm, LANES), row)] * 4,
        out_specs=[pl.BlockSpec((tm, GROUP_W), row)] * 5,
        compiler_params=_cparams(("arbitrary",)),
        name="proj",
    )(x2, ada3, ada3, w_perm, ca, sa, ci, si)


def _mla_prep_kernel(sm_ref, ci, si, cm, sm, qg, kvg, ikg, ikb, wq_ref, wkv_ref,
                     qb_o, kb_o, vb_o, iklo_o, ikhi_o, iw_o):
    small = sm_ref[...]
    qd = small[:, :Q_LORA]
    kvd = small[:, Q_LORA:Q_LORA + KV_LORA]
    ikw = small[:, Q_LORA + KV_LORA:Q_LORA + KV_LORA + LANES]
    kr = small[:, Q_LORA + KV_LORA + LANES:]

    lane = lax.broadcasted_iota(jnp.int32, ikw.shape, 1)
    is_k = lane < IDX_DIM
    ik = jnp.where(is_k, ikw, 0.0)
    mu = jnp.sum(ik, axis=-1, keepdims=True) * (1.0 / IDX_DIM)
    dk = jnp.where(is_k, ikw - mu, 0.0)
    var = jnp.sum(dk * dk, axis=-1, keepdims=True) * (1.0 / IDX_DIM)
    ik = dk * lax.rsqrt(var + EPS) * ikg[...] + ikb[...]
    ik = _rope128(ik, ci[...], si[...], IDX_ROT_DIM // 2, IDX_DIM)
    ik = jnp.where(is_k, ik, 0.0)
    iklo_o[...] = ik.astype(iklo_o.dtype)
    ikhi_o[...] = pltpu.roll(ik, IDX_DIM, 1).astype(ikhi_o.dtype)
    iw_o[...] = pltpu.roll(ikw, LANES - IDX_DIM, 1) * INDEX_SCALE

    cos_m, sin_m = cm[...], sm[...]
    q = _dot(_rms(qd, qg[...]), wq_ref[...])
    scale = (QK_NOPE + QK_ROPE) ** -0.5
    for h in range(B_HEADS):
        lo = slice(h * QK_PAD, h * QK_PAD + LANES)
        hi = slice(h * QK_PAD + LANES, (h + 1) * QK_PAD)
        qb_o[:, lo] = (q[:, lo] * scale).astype(qb_o.dtype)
        qb_o[:, hi] = (_rope128(q[:, hi], cos_m, sin_m, QK_ROPE // 2, LANES) * scale).astype(qb_o.dtype)

    kv = _dot(_rms(kvd, kvg[...]), wkv_ref[...])
    krr = _rope128(kr, cos_m, sin_m, QK_ROPE // 2, LANES).astype(kb_o.dtype)
    for h in range(B_HEADS):
        kb_o[:, h * QK_PAD:h * QK_PAD + LANES] = kv[:, h * QK_NOPE:(h + 1) * QK_NOPE].astype(kb_o.dtype)
        kb_o[:, h * QK_PAD + LANES:(h + 1) * QK_PAD] = krr
    vb_o[...] = kv[:, B_HEADS * QK_NOPE:].astype(vb_o.dtype)


def _mla_prep(small, ci, si, cm, sm, qg, kvg, ikg, ikb, wq, wkv):
    n = small.shape[0]
    tm = _tile(n, 1024)
    row = lambda i: (i, 0)
    const = lambda a: pl.BlockSpec(a.shape, lambda i: (0, 0))
    tab = pl.BlockSpec((tm, LANES), row)
    wide = B_HEADS * QK_PAD
    return pl.pallas_call(
        _mla_prep_kernel,
        out_shape=[jax.ShapeDtypeStruct((n, wide), _MXU_DTYPE), jax.ShapeDtypeStruct((n, wide), _MXU_DTYPE),
                   jax.ShapeDtypeStruct((n, B_WIDTH), _MXU_DTYPE),
                   jax.ShapeDtypeStruct((n, LANES), _MXU_DTYPE), jax.ShapeDtypeStruct((n, LANES), _MXU_DTYPE),
                   jax.ShapeDtypeStruct((n, LANES), jnp.float32)],
        grid=(n // tm,),
        in_specs=[pl.BlockSpec((tm, GROUP_W), row), tab, tab, tab, tab,
                  const(qg), const(kvg), const(ikg), const(ikb), const(wq), const(wkv)],
        out_specs=[pl.BlockSpec((tm, wide), row), pl.BlockSpec((tm, wide), row), pl.BlockSpec((tm, B_WIDTH), row),
                   tab, tab, tab],
        compiler_params=_cparams(("arbitrary",)),
        name="mla_prep",
    )(small, ci, si, cm, sm, qg, kvg, ikg, ikb, wq, wkv)


def _indexer_kernel(iq_ref, iklo_ref, ikhi_ref, iw_ref, mask_o, key_scr, *, tq, tk, topk):
    i = pl.program_id(1)
    n_kb_all = mask_o.shape[1]
    n_kb = ((i + 1) * tq + tk - 1) // tk
    q_chunk = (i * tq + lax.broadcasted_iota(jnp.int32, (tq, tk), 0)) // CHUNK
    col = lax.broadcasted_iota(jnp.int32, (tq, tk), 1)
    iw = iw_ref[0]

    def score_block(kb, carry):
        klo = iklo_ref[0, pl.ds(kb * tk, tk), :]
        khi = ikhi_ref[0, pl.ds(kb * tk, tk), :]
        sc = jnp.zeros((tq, tk), jnp.float32)
        for p in range(IDX_HEADS // 2):
            qp = iq_ref[0, :, p * LANES:(p + 1) * LANES]
            for half, kk in ((0, klo), (1, khi)):
                hd = 2 * p + half
                sc = sc + jnp.maximum(_dot_nt(qp, kk), 0.0) * iw[:, hd:hd + 1]
        bits = pltpu.bitcast(sc, jnp.int32)
        key = jnp.where(bits < 0, bits ^ 0x7FFFFFFF, bits)
        visible = (kb * tk + col) // CHUNK <= q_chunk
        key_scr[kb] = jnp.where(visible, key, INT_MIN)
        return carry

    lax.fori_loop(0, n_kb, score_block, 0)

    def search_cond(state):
        it, _, settled = state
        return jnp.logical_and(it < 32, jnp.min(settled) == 0)

    def bit_pass(state):
        it, t_u, settled = state
        cand_u = t_u | lax.shift_left(jnp.int32(1), 31 - it)
        cand_s = cand_u ^ INT_MIN

        def count_block(kb, acc):
            blk = key_scr[kb]
            for c in range(tk // LANES):
                acc = acc + jnp.where(blk[:, c * LANES:(c + 1) * LANES] >= cand_s, 1.0, 0.0)
            return acc

        acc = lax.fori_loop(0, n_kb, count_block, jnp.zeros((tq, LANES), jnp.float32))
        cnt = jnp.sum(acc, axis=1, keepdims=True)
        open_row = settled == 0
        t_u = jnp.where(jnp.logical_and(open_row, cnt >= topk), cand_u, t_u)
        settled = jnp.where(jnp.logical_and(open_row, cnt == topk), 1, settled)
        return it + 1, t_u, settled

    state = (jnp.int32(0), jnp.zeros((tq, 1), jnp.int32), jnp.zeros((tq, 1), jnp.int32))
    state = lax.fori_loop(0, UNCHECKED_PASSES, lambda _, st: bit_pass(st), state)
    _, t_u, _ = lax.while_loop(search_cond, bit_pass, state)
    t_s = jnp.maximum(t_u ^ INT_MIN, INT_MIN + 1)

    for kb in range(n_kb_all):
        @pl.when(kb < n_kb)
        def _():
            mask_o[0, kb] = jnp.where(key_scr[kb] >= t_s, 0.0, NEG_BIG).astype(mask_o.dtype)

        @pl.when(kb >= n_kb)
        def _():
            mask_o[0, kb] = jnp.full((tq, tk), NEG_BIG, mask_o.dtype)


def _indexer(iq3, iklo3, ikhi3, iw3, topk):
    b, s, _ = iq3.shape
    tq = _tile(s, 128)
    tk = _tile(s, KEY_BLOCK)
    return pl.pallas_call(
        functools.partial(_indexer_kernel, tq=tq, tk=tk, topk=topk),
        out_shape=jax.ShapeDtypeStruct((b, s // tk, s, tk), jnp.bfloat16),
        grid=(b, s // tq),
        in_specs=[pl.BlockSpec((1, tq, GROUP_W), lambda bi, i: (bi, i, 0)),
                  pl.BlockSpec((1, s, LANES), lambda bi, i: (bi, 0, 0)),
                  pl.BlockSpec((1, s, LANES), lambda bi, i: (bi, 0, 0)),
                  pl.BlockSpec((1, tq, LANES), lambda bi, i: (bi, i, 0))],
        out_specs=pl.BlockSpec((1, s // tk, tq, tk), lambda bi, i: (bi, 0, i, 0)),
        scratch_shapes=[pltpu.VMEM((s // tk, tq, tk), jnp.int32)],
        compiler_params=_cparams(("arbitrary", "arbitrary")),
        name="indexer",
    )(iq3, iklo3, ikhi3, iw3)


def _softmax_heads(q_ref, k_ref, v_ref, heads, dqk, dv, n_plain, n_kb, tk, bias_fn, s_scr):
    tq = q_ref.shape[1]
    lane_tiles = [slice(c * LANES, (c + 1) * LANES) for c in range(tk // LANES)]

    def scores(masked):
        def body(kb, mxs):
            rows = pl.ds(pl.multiple_of(kb * tk, tk), tk)
            out = []
            for g, h in enumerate(heads):
                s = _dot_nt(q_ref[0, :, h * dqk:(h + 1) * dqk], k_ref[0, rows, h * dqk:(h + 1) * dqk])
                if masked:
                    s = bias_fn(kb, s)
                s_scr[g, kb] = s
                mx = mxs[g]
                for sl in lane_tiles:
                    mx = jnp.maximum(mx, s[:, sl])
                out.append(mx)
            return tuple(out)
        return body

    mxs = tuple(jnp.full((tq, LANES), NEG_BIG, jnp.float32) for _ in heads)
    mxs = lax.fori_loop(0, n_plain, scores(False), mxs)
    mxs = lax.fori_loop(n_plain, n_kb, scores(True), mxs)
    ms = [jnp.max(mx, axis=1, keepdims=True) for mx in mxs]

    def weigh(kb, carry):
        rows = pl.ds(pl.multiple_of(kb * tk, tk), tk)
        out = []
        for g, h in enumerate(heads):
            l, acc = carry[g]
            p = jnp.exp(s_scr[g, kb] - ms[g])
            for sl in lane_tiles:
                l = l + p[:, sl]
            out.append((l, acc + _dot(p, v_ref[0, rows, h * dv:(h + 1) * dv])))
        return tuple(out)

    zero = (jnp.zeros((tq, LANES), jnp.float32), jnp.zeros((tq, dv), jnp.float32))
    res = lax.fori_loop(0, n_kb, weigh, tuple(zero for _ in heads))
    return [acc / jnp.sum(l, axis=1, keepdims=True) for l, acc in res]


def _attn_a_kernel(q_ref, k_ref, v_ref, bias_ref, g_ref, o_ref, o_scr, s_scr, *, tq, tk):
    i = pl.program_id(1)
    n_kb = ((i + 1) * tq + tk - 1) // tk

    def bias_fn(kb, s):
        return s + bias_ref[0, kb].astype(jnp.float32)

    for h0 in range(0, A_HEADS, HEAD_GROUP):
        heads = list(range(h0, h0 + HEAD_GROUP))
        outs = _softmax_heads(q_ref, k_ref, v_ref, heads, A_HEAD_DIM, A_HEAD_DIM, 0, n_kb, tk, bias_fn, s_scr)
        for h, o in zip(heads, outs):
            o_scr[:, h * A_HEAD_DIM:(h + 1) * A_HEAD_DIM] = o
    o_ref[0] = _rms(o_scr[...], g_ref[...]).astype(o_ref.dtype)


def _attn_a(q3, k3, v3, mask4, g):
    b, s, w = q3.shape
    tk = mask4.shape[3]
    tq = _tile(s, 256)
    full = lambda bi, i: (bi, 0, 0)
    return pl.pallas_call(
        functools.partial(_attn_a_kernel, tq=tq, tk=tk),
        out_shape=jax.ShapeDtypeStruct((b, s, w), _MXU_DTYPE),
        grid=(b, s // tq),
        in_specs=[pl.BlockSpec((1, tq, w), lambda bi, i: (bi, i, 0)),
                  pl.BlockSpec((1, s, w), full, pipeline_mode=pl.Buffered(1)),
                  pl.BlockSpec((1, s, w), full, pipeline_mode=pl.Buffered(1)),
                  pl.BlockSpec((1, s // tk, tq, tk), lambda bi, i: (bi, 0, i, 0)),
                  pl.BlockSpec((1, w), lambda bi, i: (0, 0))],
        out_specs=pl.BlockSpec((1, tq, w), lambda bi, i: (bi, i, 0)),
        scratch_shapes=[pltpu.VMEM((tq, w), jnp.float32), pltpu.VMEM((HEAD_GROUP, s // tk, tq, tk), jnp.float32)],
        compiler_params=_cparams(("arbitrary", "arbitrary")),
        name="attn_a",
    )(q3, k3, v3, mask4, g)


def _attn_b_kernel(q_ref, k_ref, v_ref, g_ref, o_ref, o_scr, s_scr, *, tq, tk):
    i = pl.program_id(1)
    n_kb = ((i + 1) * tq + tk - 1) // tk
    n_plain = (i * tq) // tk
    q_chunk = (i * tq + lax.broadcasted_iota(jnp.int32, (tq, tk), 0)) // CHUNK
    col = lax.broadcasted_iota(jnp.int32, (tq, tk), 1)

    def bias_fn(kb, s):
        return jnp.where((kb * tk + col) // CHUNK <= q_chunk, s, NEG_BIG)

    for h0 in range(0, B_HEADS, HEAD_GROUP):
        heads = list(range(h0, h0 + HEAD_GROUP))
        outs = _softmax_heads(q_ref, k_ref, v_ref, heads, QK_PAD, V_HEAD, n_plain, n_kb, tk, bias_fn, s_scr)
        for h, o in zip(heads, outs):
            o_scr[:, h * V_HEAD:(h + 1) * V_HEAD] = o
    o_ref[0] = _rms(o_scr[...], g_ref[...]).astype(o_ref.dtype)


def _attn_b(q3, k3, v3, g):
    b, s, wq = q3.shape
    wv = v3.shape[2]
    tq = _tile(s, 256)
    tk = _tile(s, KEY_BLOCK)
    full = lambda bi, i: (bi, 0, 0)
    return pl.pallas_call(
        functools.partial(_attn_b_kernel, tq=tq, tk=tk),
        out_shape=jax.ShapeDtypeStruct((b, s, wv), _MXU_DTYPE),
        grid=(b, s // tq),
        in_specs=[pl.BlockSpec((1, tq, wq), lambda bi, i: (bi, i, 0)),
                  pl.BlockSpec((1, s, wq), full, pipeline_mode=pl.Buffered(1)),
                  pl.BlockSpec((1, s, wv), full, pipeline_mode=pl.Buffered(1)),
                  pl.BlockSpec((1, wv), lambda bi, i: (0, 0))],
        out_specs=pl.BlockSpec((1, tq, wv), lambda bi, i: (bi, i, 0)),
        scratch_shapes=[pltpu.VMEM((tq, wv), jnp.float32), pltpu.VMEM((HEAD_GROUP, s // tk, tq, tk), jnp.float32)],
        compiler_params=_cparams(("arbitrary", "arbitrary")),
        name="attn_b",
    )(q3, k3, v3, g)


def _outproj_kernel(ma_ref, mb_ref, w_ref, x_ref, g1_ref, sc2_ref, sh2_ref, lg_ref, lb_ref, wrh_ref, wrl_ref, br_ref,
                    x1_o, h2_o, lgt_o, *, alpha, n_experts, n_split):
    rows_per = x_ref.shape[0] // n_split
    lane = lax.broadcasted_iota(jnp.int32, (rows_per, LANES), 1)
    for part in range(n_split):
        r = slice(part * rows_per, (part + 1) * rows_per)
        mix = (jnp.dot(ma_ref[r, :], w_ref[:A_WIDTH, :], preferred_element_type=jnp.float32)
               + jnp.dot(mb_ref[r, :], w_ref[A_WIDTH:, :], preferred_element_type=jnp.float32))
        x1 = _ln_plain(alpha * x_ref[r, :] + g1_ref[0] * mix) * lg_ref[...] + lb_ref[...]
        x1_o[r, :] = x1
        h2 = _ln_plain(x1) * (1.0 + sc2_ref[0]) + sh2_ref[0]
        h2_o[r, :] = h2
        h_hi = h2.astype(_MXU_DTYPE)
        h_lo = (h2 - h_hi.astype(jnp.float32)).astype(_MXU_DTYPE)
        logits = (jnp.dot(h_hi, wrh_ref[...], preferred_element_type=jnp.float32)
                  + jnp.dot(h_lo, wrh_ref[...], preferred_element_type=jnp.float32)
                  + jnp.dot(h_hi, wrl_ref[...], preferred_element_type=jnp.float32)) + br_ref[...]
        lgt_o[r, :] = jnp.where(lane < n_experts, logits, -jnp.inf)


def _outproj(ma, mb, w_out, x2, ada3, lg, lb, wr, br, seq, alpha, n_experts):
    n, d = x2.shape
    tm = _tile(seq, 512)
    n_split = 2 if tm % 32 == 0 else 1
    wr_hi = wr.astype(_MXU_DTYPE)
    wr_lo = (wr - wr_hi.astype(jnp.float32)).astype(_MXU_DTYPE)
    row = lambda i: (i, 0)
    const = lambda a: pl.BlockSpec(a.shape, lambda i: (0,) * a.ndim)
    ada_spec = lambda col: pl.BlockSpec((1, 1, d), lambda i, col=col: (i * tm // seq, 0, col))
    return pl.pallas_call(
        functools.partial(_outproj_kernel, alpha=alpha, n_experts=n_experts, n_split=n_split),
        out_shape=[jax.ShapeDtypeStruct((n, d), jnp.float32),
                   jax.ShapeDtypeStruct((n, d), jnp.float32),
                   jax.ShapeDtypeStruct((n, LANES), jnp.float32)],
        grid=(n // tm,),
        in_specs=[pl.BlockSpec((tm, A_WIDTH), row), pl.BlockSpec((tm, B_WIDTH), row),
                  pl.BlockSpec(w_out.shape, lambda i: (0, 0), pipeline_mode=pl.Buffered(1)),
                  pl.BlockSpec((tm, d), row), ada_spec(2), ada_spec(4), ada_spec(3),
                  const(lg), const(lb), const(wr_hi), const(wr_lo), const(br)],
        out_specs=[pl.BlockSpec((tm, d), row), pl.BlockSpec((tm, d), row), pl.BlockSpec((tm, LANES), row)],
        compiler_params=_cparams(("arbitrary",)),
        name="outproj",
    )(ma, mb, w_out, x2, ada3, ada3, ada3, lg, lb, wr_hi, wr_lo, br)


def _route_kernel(lgt_ref, top_o, gate_o, rank_o, cnt_o, carry):
    tb = lgt_ref.shape[0]

    @pl.when(pl.program_id(0) == 0)
    def _():
        carry[...] = jnp.zeros_like(carry)

    lane = lax.broadcasted_iota(jnp.int32, (tb, LANES), 1)
    lane_f = lane.astype(jnp.float32)
    work = lgt_ref[...]
    vals, idxs, hots = [], [], []
    for _ in range(TOP_K_EXPERTS):
        m = jnp.max(work, axis=1, keepdims=True)
        idx = jnp.min(jnp.where(work == m, lane_f, float(LANES)), axis=1, keepdims=True)
        hot = lane_f == idx
        vals.append(m)
        idxs.append(idx)
        hots.append(hot)
        work = jnp.where(hot, -jnp.inf, work)

    exps = [jnp.exp(v - vals[0]) for v in vals]
    denom = exps[0] + exps[1] + exps[2] + exps[3]

    member = jnp.zeros((tb, LANES), jnp.float32)
    for hot in hots:
        member = member + jnp.where(hot, 1.0, 0.0)
    r = lax.broadcasted_iota(jnp.int32, (tb, tb), 0)
    c = lax.broadcasted_iota(jnp.int32, (tb, tb), 1)
    before = jnp.where(c < r, 1.0, 0.0).astype(jnp.bfloat16)
    prefix = jnp.dot(before, member.astype(jnp.bfloat16), preferred_element_type=jnp.float32) + carry[...]

    top = jnp.zeros((tb, LANES), jnp.int32)
    gate = jnp.zeros((tb, LANES), jnp.float32)
    rank = jnp.zeros((tb, LANES), jnp.int32)
    for k in range(TOP_K_EXPERTS):
        rk = jnp.sum(jnp.where(hots[k], prefix, 0.0), axis=1, keepdims=True)
        top = jnp.where(lane == k, idxs[k].astype(jnp.int32), top)
        gate = jnp.where(lane == k, exps[k] / denom, gate)
        rank = jnp.where(lane == k, rk.astype(jnp.int32), rank)
    top_o[...] = top
    gate_o[...] = gate
    rank_o[...] = rank
    carry[...] = carry[...] + jnp.sum(member, axis=0, keepdims=True)
    cnt_o[...] = jnp.broadcast_to(carry[...], cnt_o.shape).astype(jnp.int32)


def _route(logits):
    n = logits.shape[0]
    tb = _tile(n, 512)
    row = pl.BlockSpec((tb, LANES), lambda i: (i, 0))
    return pl.pallas_call(
        _route_kernel,
        out_shape=[jax.ShapeDtypeStruct((n, LANES), jnp.int32), jax.ShapeDtypeStruct((n, LANES), jnp.float32),
                   jax.ShapeDtypeStruct((n, LANES), jnp.int32), jax.ShapeDtypeStruct((8, LANES), jnp.int32)],
        grid=(n // tb,),
        in_specs=[row],
        out_specs=[row, row, row, pl.BlockSpec((8, LANES), lambda i: (0, 0))],
        scratch_shapes=[pltpu.VMEM((1, LANES), jnp.float32)],
        compiler_params=_cparams(("arbitrary",)),
        name="route",
    )(logits)


def _row_copy(src, s_row, dst, d_row, sem):
    return pltpu.make_async_copy(src.at[pl.ds(s_row, 1), :], dst.at[pl.ds(d_row, 1), :], sem)


def _dispatch_kernel(dest_ref, fill_ref, h_ref, xs_hbm, zeros, sem, zsem, *, td, sub, n_experts):
    base = pl.program_id(0) * td

    @pl.when(pl.program_id(0) == 0)
    def _():
        zeros[...] = jnp.zeros_like(zeros)
        tail0 = fill_ref[2 * n_experts]
        n_tail = fill_ref[2 * n_experts + 1]

        def tail_copy(j):
            rows = pl.ds(pl.multiple_of(tail0 + j * sub, sub), sub)
            return pltpu.make_async_copy(zeros, xs_hbm.at[rows, :], zsem)

        def for_pad_rows(fn):
            def per_expert(e, c):
                first = fill_ref[e]
                lax.fori_loop(0, fill_ref[n_experts + e], lambda r, c2: (fn(first + r), c2)[1], 0)
                return c
            lax.fori_loop(0, n_experts, per_expert, 0)

        for_pad_rows(lambda r: _row_copy(zeros, 0, xs_hbm, r, zsem).start())
        lax.fori_loop(0, n_tail, lambda j, c: (tail_copy(j).start(), c)[1], 0)
        for_pad_rows(lambda r: _row_copy(zeros, 0, xs_hbm, r, zsem).wait())
        lax.fori_loop(0, n_tail, lambda j, c: (tail_copy(j).wait(), c)[1], 0)

    def issue(t, c):
        for k in range(TOP_K_EXPERTS):
            _row_copy(h_ref, t, xs_hbm, dest_ref[(base + t) * TOP_K_EXPERTS + k], sem).start(priority=k % 2)
        return c

    lax.fori_loop(0, td, issue, 0, unroll=ROW_DMA_UNROLL)
    landed = xs_hbm.at[pl.ds(0, td * TOP_K_EXPERTS), :]
    pltpu.make_async_copy(landed, landed, sem).wait()


def _dispatch(dest_flat, fill, h2, rows, sub, n_experts):
    n, d = h2.shape
    td = _tile(n, 512)
    return pl.pallas_call(
        functools.partial(_dispatch_kernel, td=td, sub=sub, n_experts=n_experts),
        out_shape=jax.ShapeDtypeStruct((rows, d), h2.dtype),
        grid_spec=pltpu.PrefetchScalarGridSpec(
            num_scalar_prefetch=2, grid=(n // td,),
            in_specs=[pl.BlockSpec((td, d), lambda i, dr, fl: (i, 0))],
            out_specs=pl.BlockSpec(memory_space=pl.ANY),
            scratch_shapes=[pltpu.VMEM((sub, d), h2.dtype), pltpu.SemaphoreType.DMA(()),
                            pltpu.SemaphoreType.DMA(())]),
        compiler_params=pltpu.CompilerParams(dimension_semantics=("arbitrary",), has_side_effects=True,
                                             disable_bounds_checks=True),
        name="dispatch",
    )(dest_flat, fill, h2)


def _swiglu_compact(gu):
    tf2 = gu.shape[1]
    nxt = pltpu.roll(gu, tf2 - 1, 1)
    g = jnp.minimum(gu, SWIGLU_LIMIT)
    u = jnp.clip(nxt, -SWIGLU_LIMIT, SWIGLU_LIMIT)
    act = ((u + 1.0) * (g * jax.nn.sigmoid(SWIGLU_ALPHA * g))).astype(_MXU_DTYPE)
    r = lax.broadcasted_iota(jnp.int32, (2 * LANES, LANES), 0)
    c = lax.broadcasted_iota(jnp.int32, (2 * LANES, LANES), 1)
    pick = jnp.where(r == 2 * c, 1.0, 0.0).astype(_MXU_DTYPE)
    parts = [jnp.dot(act[:, t * 2 * LANES:(t + 1) * 2 * LANES], pick, preferred_element_type=jnp.float32)
             for t in range(tf2 // (2 * LANES))]
    return jnp.concatenate(parts, axis=1).astype(_MXU_DTYPE)


def _moe_gemm_kernel(sbe_ref, row0_ref, nsub_ref, xs_hbm, wgu_ref, bgu_ref, wd_ref, bd_ref, y_hbm,
                     xstage, xb, act, wb, wdb, ystage, gu_scr, sem_x, sem_y, *, sub, n_f):
    del sbe_ref
    sb = pl.program_id(0)
    t = pl.program_id(1)
    nsub = nsub_ref[sb]
    row0 = row0_ref[sb]
    tf = wd_ref.shape[1]

    def x_copy(c, slot):
        rows = pl.ds(pl.multiple_of(row0 + c * sub, sub), sub)
        return pltpu.make_async_copy(xs_hbm.at[rows, :], xstage.at[slot], sem_x.at[slot])

    def y_copy(s, slot):
        rows = pl.ds(pl.multiple_of(row0 + s * sub, sub), sub)
        return pltpu.make_async_copy(ystage.at[slot], y_hbm.at[rows, :], sem_y.at[slot])

    @pl.when(jnp.logical_and(nsub > 0, t < n_f))
    def _():
        wb[...] = wgu_ref[0].astype(wb.dtype)
        wdb[pl.ds(pl.multiple_of(t * tf, tf), tf), :] = wd_ref[0].astype(wdb.dtype)
        bias = bgu_ref[0]

        def gate_up(x):
            return jnp.dot(x, wb[...], preferred_element_type=jnp.float32) + bias

        @pl.when(t == 0)
        def _():
            x_copy(0, 0).start()

            def first(c, carry):
                slot = c % 2

                @pl.when(c + 1 < nsub)
                def _():
                    x_copy(c + 1, 1 - slot).start()

                x_copy(c, slot).wait()
                rows = pl.ds(pl.multiple_of(c * sub, sub), sub)
                xv = xstage[slot].astype(xb.dtype)
                xb[rows, :] = xv
                act[0, rows, :] = _swiglu_compact(gate_up(xv))
                return carry

            lax.fori_loop(0, nsub, first, 0)

        @pl.when(t > 0)
        def _():
            n_pairs = nsub // 2
            pair_rows = lambda p: pl.ds(pl.multiple_of(p * 2 * sub, 2 * sub), 2 * sub)

            @pl.when(n_pairs > 0)
            def _():
                gu_scr[...] = gate_up(xb[pair_rows(0), :])

                def up(p, carry):
                    prev = gu_scr[...]
                    gu_scr[...] = gate_up(xb[pair_rows(p), :])
                    act[t, pair_rows(p - 1), :] = _swiglu_compact(prev)
                    return carry

                lax.fori_loop(1, n_pairs, up, 0)
                act[t, pair_rows(n_pairs - 1), :] = _swiglu_compact(gu_scr[...])

            @pl.when(nsub % 2 == 1)
            def _():
                rows = pl.ds(pl.multiple_of((nsub - 1) * sub, sub), sub)
                act[t, rows, :] = _swiglu_compact(gate_up(xb[rows, :]))

    @pl.when(jnp.logical_and(nsub > 0, t == n_f))
    def _():
        bias = bd_ref[0]

        def down(s, carry):
            slot = s % 2
            rows = pl.ds(pl.multiple_of(s * sub, sub), sub)
            a = jnp.concatenate([act[f, rows, :] for f in range(n_f)], axis=1)
            yt = jnp.dot(a, wdb[...], preferred_element_type=jnp.float32) + bias

            @pl.when(s >= 2)
            def _():
                y_copy(s - 2, slot).wait()

            ystage[slot] = yt
            y_copy(s, slot).start()
            return carry

        lax.fori_loop(0, nsub, down, 0)

        @pl.when(nsub >= 2)
        def _():
            y_copy(nsub - 2, nsub % 2).wait()

        y_copy(nsub - 1, (nsub - 1) % 2).wait()


def _moe_gemm(sb_e, sb_row0, sb_nsub, xs, wgu, bgu3, wd, bd3, sub, max_sub):
    e, d, ff2 = wgu.shape
    ff = ff2 // 2
    tf = _tile(ff, 256)
    n_f = ff // tf
    n_sb = sb_e.shape[0]
    rm = max_sub * sub
    f_idx = lambda sb, t, ns: jnp.where(ns[sb] > 0, jnp.minimum(t, n_f - 1), n_f - 1)
    return pl.pallas_call(
        functools.partial(_moe_gemm_kernel, sub=sub, n_f=n_f),
        out_shape=jax.ShapeDtypeStruct(xs.shape, jnp.float32),
        grid_spec=pltpu.PrefetchScalarGridSpec(
            num_scalar_prefetch=3, grid=(n_sb, n_f + 1),
            in_specs=[pl.BlockSpec(memory_space=pl.ANY),
                      pl.BlockSpec((1, d, 2 * tf), lambda sb, t, se, r0, ns: (se[sb], 0, f_idx(sb, t, ns))),
                      pl.BlockSpec((1, 1, 2 * tf), lambda sb, t, se, r0, ns: (se[sb], 0, f_idx(sb, t, ns))),
                      pl.BlockSpec((1, tf, d), lambda sb, t, se, r0, ns: (se[sb], f_idx(sb, t, ns), 0)),
                      pl.BlockSpec((1, 1, d), lambda sb, t, se, r0, ns: (se[sb], 0, 0))],
            out_specs=pl.BlockSpec(memory_space=pl.ANY),
            scratch_shapes=[pltpu.VMEM((2, sub, d), jnp.float32),
                            pltpu.VMEM((rm, d), _MXU_DTYPE),
                            pltpu.VMEM((n_f, rm, tf), _MXU_DTYPE),
                            pltpu.VMEM((d, 2 * tf), _MXU_DTYPE),
                            pltpu.VMEM((ff, d), _MXU_DTYPE),
                            pltpu.VMEM((2, sub, d), jnp.float32),
                            pltpu.VMEM((2 * sub, 2 * tf), jnp.float32),
                            pltpu.SemaphoreType.DMA((2,)), pltpu.SemaphoreType.DMA((2,))]),
        input_output_aliases={3: 0},
        compiler_params=_cparams(("arbitrary", "arbitrary")),
        name="moe_gemm",
    )(sb_e, sb_row0, sb_nsub, xs, wgu, bgu3, wd, bd3)


def _combine_kernel(dest_ref, y_hbm, gate_ref, x1_ref, g2_ref, lg_ref, lb_ref, o_ref, buf, sem, *, tc, alpha):
    base = pl.program_id(0) * tc

    def issue(t, c):
        for k in range(TOP_K_EXPERTS):
            _row_copy(y_hbm, dest_ref[(base + t) * TOP_K_EXPERTS + k], buf, k * tc + t, sem).start()
        return c

    lax.fori_loop(0, tc, issue, 0, unroll=ROW_DMA_UNROLL)
    pltpu.make_async_copy(y_hbm.at[pl.ds(0, tc * TOP_K_EXPERTS), :], buf, sem).wait()

    gate = gate_ref[...]
    ffn = buf[0:tc, :] * gate[:, 0:1]
    for k in range(1, TOP_K_EXPERTS):
        ffn = ffn + buf[k * tc:(k + 1) * tc, :] * gate[:, k:k + 1]
    o_ref[...] = _ln_plain(alpha * x1_ref[...] + g2_ref[0] * ffn) * lg_ref[...] + lb_ref[...]


def _combine(dest_flat, y, gates, x1, ada3, lg, lb, seq, alpha):
    n, d = x1.shape
    tc = _tile(seq, 512)
    row = lambda i, dr: (i, 0)
    return pl.pallas_call(
        functools.partial(_combine_kernel, tc=tc, alpha=alpha),
        out_shape=jax.ShapeDtypeStruct((n, d), jnp.float32),
        grid_spec=pltpu.PrefetchScalarGridSpec(
            num_scalar_prefetch=1, grid=(n // tc,),
            in_specs=[pl.BlockSpec(memory_space=pl.ANY),
                      pl.BlockSpec((tc, LANES), row), pl.BlockSpec((tc, d), row),
                      pl.BlockSpec((1, 1, d), lambda i, dr: (i * tc // seq, 0, 5)),
                      pl.BlockSpec((1, d), lambda i, dr: (0, 0)), pl.BlockSpec((1, d), lambda i, dr: (0, 0))],
            out_specs=pl.BlockSpec((tc, d), row),
            scratch_shapes=[pltpu.VMEM((TOP_K_EXPERTS * tc, d), jnp.float32),
                            pltpu.SemaphoreType.DMA(())]),
        compiler_params=pltpu.CompilerParams(dimension_semantics=("arbitrary",), vmem_limit_bytes=VMEM_LIMIT,
                                             disable_bounds_checks=True),
        name="combine",
    )(dest_flat, y, gates, x1, ada3, lg, lb)


def _pad_cols(a, width):
    return jnp.pad(a, ((0, 0), (0, width - a.shape[1])))


def _perm_w_in(w_in):
    o = np.cumsum([0, A_WIDTH, A_WIDTH, A_WIDTH, IDX_HEADS * IDX_DIM, IDX_DIM, IDX_HEADS, Q_LORA, KV_LORA, QK_ROPE])
    seg = lambda i: w_in[:, o[i]:o[i + 1]]
    ikw = _pad_cols(jnp.concatenate([seg(4), seg(5)], axis=1), LANES)
    kr = _pad_cols(seg(8), LANES)
    return jnp.concatenate([seg(0), seg(1), seg(2), seg(3), seg(6), seg(7), ikw, kr], axis=1)


def _perm_w_q_up(w):
    w = w.reshape(Q_LORA, B_HEADS, QK_NOPE + QK_ROPE)
    w = jnp.pad(w, ((0, 0), (0, 0), (0, QK_PAD - QK_NOPE - QK_ROPE)))
    return w.reshape(Q_LORA, B_HEADS * QK_PAD)


def _perm_w_kv_up(w):
    w = w.reshape(KV_LORA, B_HEADS, QK_NOPE + V_HEAD)
    return jnp.concatenate([w[:, :, :QK_NOPE].reshape(KV_LORA, -1), w[:, :, QK_NOPE:].reshape(KV_LORA, -1)], axis=1)


MOE_MAX_SUB = 9


def _moe_sub_rows(n_tokens):
    return 256 if n_tokens >= 4096 else 64


def _super_blocks(cnt, sub, n_assign):
    n_experts = cnt.shape[0]
    q = (cnt + sub - 1) // sub
    pad_ends = jnp.cumsum(q * sub)
    pad_starts = pad_ends - q * sub
    nsb_e = (q + MOE_MAX_SUB - 1) // MOE_MAX_SUB
    sb_end = jnp.cumsum(nsb_e)
    sb_start = sb_end - nsb_e
    n_sb = (n_assign // sub + n_experts + MOE_MAX_SUB - 1) // MOE_MAX_SUB + n_experts
    idx = jnp.arange(n_sb, dtype=jnp.int32)
    valid = idx < sb_end[-1]
    e_of = jnp.minimum(jnp.sum((sb_end[None, :] <= idx[:, None]).astype(jnp.int32), axis=1), n_experts - 1)
    j = idx - sb_start[e_of]
    parts = jnp.maximum(nsb_e[e_of], 1)
    base, rem = q[e_of] // parts, q[e_of] % parts
    nsub = jnp.where(valid, base + (j < rem).astype(jnp.int32), 0)
    row0 = jnp.where(valid, pad_starts[e_of] + (j * base + jnp.minimum(j, rem)) * sub, 0)
    e_last = jnp.max(jnp.where(valid, e_of, 0))
    sb_e = jnp.where(valid, e_of, e_last)
    i32 = lambda a: a.astype(jnp.int32)
    rows = (n_assign // sub + n_experts) * sub
    fill = jnp.concatenate([pad_starts + cnt, q * sub - cnt, jnp.stack([pad_ends[-1], (rows - pad_ends[-1]) // sub])])
    return pad_starts, i32(sb_e), i32(row0), i32(nsub), i32(fill)


def kernel(x, c, positions, w_ada, b_ada, w_in, idx_k_norm_g, idx_k_norm_b, q_norm_g, w_q_up, kv_norm_g, w_kv_up,
           out_norm_a_g, out_norm_b_g, w_out, ln_mix_g, ln_mix_b, w_router, b_router, w_gate_up, b_gate_up,
           w_down, b_down, ln_ffn_g, ln_ffn_b):
    bsz, seq, d = x.shape
    depth = w_ada.shape[0]
    n_experts = w_router.shape[2]
    n = bsz * seq
    alpha = (2 * depth) ** 0.25
    topk = min(TOPK_MAX, seq // 4)
    sub = _moe_sub_rows(n)
    n_assign = n * TOP_K_EXPERTS
    rows = (n_assign // sub + n_experts) * sub

    ca, sa, ci, si, cm, sm = _rope_tables(positions.reshape(n, 1))
    c8 = jnp.pad(c, ((0, 8 - bsz), (0, 0)))
    x2 = x.reshape(n, d)
    row = lambda a: a.reshape(1, -1)

    for l in range(depth):
        ada3 = _ada(c8, w_ada[l], row(b_ada[l]))[:bsz].reshape(bsz, 1, 6 * d)

        aq, ak, av, iq, small = _proj(x2, ada3, _perm_w_in(w_in[l]).astype(_MXU_DTYPE), ca, sa, ci, si, seq)
        qb, kb, vb, iklo, ikhi, iw = _mla_prep(
            small, ci, si, cm, sm, row(q_norm_g[l]), row(kv_norm_g[l]),
            _pad_cols(row(idx_k_norm_g[l]), LANES), _pad_cols(row(idx_k_norm_b[l]), LANES),
            _perm_w_q_up(w_q_up[l]).astype(_MXU_DTYPE), _perm_w_kv_up(w_kv_up[l]).astype(_MXU_DTYPE))
        b3 = lambda a: a.reshape(bsz, seq, a.shape[-1])
        mask = _indexer(b3(iq), b3(iklo), b3(ikhi), b3(iw), topk)
        out_a = _attn_a(b3(aq), b3(ak), b3(av), mask, row(out_norm_a_g[l]))
        out_b = _attn_b(b3(qb), b3(kb), b3(vb), row(out_norm_b_g[l]))

        x1, h2, logits = _outproj(
            out_a.reshape(n, A_WIDTH), out_b.reshape(n, B_WIDTH), w_out[l].astype(_MXU_DTYPE), x2, ada3,
            row(ln_mix_g[l]), row(ln_mix_b[l]), _pad_cols(w_router[l], LANES), _pad_cols(row(b_router[l]), LANES),
            seq, alpha, n_experts)
        top, gates, rank, counts = _route(logits)

        pad_starts, sb_e, sb_row0, sb_nsub, fill = _super_blocks(counts[0, :n_experts], sub, n_assign)
        dest = (pad_starts[top[:, :TOP_K_EXPERTS]] + rank[:, :TOP_K_EXPERTS]).reshape(-1).astype(jnp.int32)

        xs = _dispatch(dest, fill, h2, rows, sub, n_experts)
        y = _moe_gemm(sb_e, sb_row0, sb_nsub, xs, w_gate_up[l], b_gate_up[l].reshape(n_experts, 1, -1), w_down[l],
                      b_down[l].reshape(n_experts, 1, -1), sub, MOE_MAX_SUB)
        x2 = _combine(dest, y, gates, x1, ada3, row(ln_ffn_g[l]), row(ln_ffn_b[l]), seq, alpha)

    return x2.reshape(bsz, seq, d)
```
